```python
import math
import jax
import jax.numpy as jnp
from jax import lax
import numpy as np

D_MODEL = 4096
BATCH = 1
SEQ = 8192
DEPTH = 4

N_BRANCH = 4
BRANCH_WIDTH = D_MODEL // 4
HGRN_HEAD_DIM = 128
HGRN_HEADS = BRANCH_WIDTH // HGRN_HEAD_DIM
HGRN_WIDTH = HGRN_HEADS * HGRN_HEAD_DIM
HGRN_CHUNK = 64
ATTN_HEAD_DIM = 64
ATTN_Q_HEADS = BRANCH_WIDTH // ATTN_HEAD_DIM
ATTN_KV_HEADS = ATTN_Q_HEADS // 4
ATTN_WINDOW = 128
ATTN_BLOCK = 128
SSD_HEAD_DIM = 64
SSD_WIDTH = BRANCH_WIDTH
SSD_HEADS = SSD_WIDTH // SSD_HEAD_DIM
SSD_GROUPS = 4
SSD_STATE = 128
SSD_CONV = 4
SSD_CHUNK = 128
SSD_XBC_WIDTH = SSD_WIDTH + 2 * SSD_GROUPS * SSD_STATE
RET_V_DIM = 128
RET_QK_DIM = 64
RET_HEADS = BRANCH_WIDTH // RET_V_DIM
RET_CHUNK = 128
GATE_RANK = 256
FFN_HIDDEN = -(-8 * D_MODEL // (3 * 256)) * 256

SPLIT_SIZES = (
    HGRN_WIDTH, HGRN_WIDTH, HGRN_WIDTH, HGRN_WIDTH,
    ATTN_Q_HEADS * ATTN_HEAD_DIM, ATTN_KV_HEADS * ATTN_HEAD_DIM,
    ATTN_KV_HEADS * ATTN_HEAD_DIM,
    SSD_WIDTH, SSD_XBC_WIDTH, SSD_HEADS,
    RET_HEADS * RET_QK_DIM, RET_HEADS * RET_QK_DIM,
    RET_HEADS * RET_V_DIM, RET_HEADS * RET_V_DIM,
    GATE_RANK,
)
IN_WIDTH = sum(SPLIT_SIZES)

kernel_name = 'hybrid_gated_parallel_mixer'


def split_columns(a, sizes):
    parts = []
    start = 0
    for size in sizes:
        parts.append(a[..., start:start + size])
        start += size
    return parts


def rms_norm(x, g, eps=1e-6):
    xf = x.astype(jnp.float32)
    y = xf * lax.rsqrt(jnp.mean(xf * xf, axis=-1, keepdims=True) + eps)
    return (y * g.astype(jnp.float32)).astype(x.dtype)


def head_group_norm(x, eps=1e-5):
    xf = x.astype(jnp.float32)
    xc = xf - jnp.mean(xf, axis=-1, keepdims=True)
    var = jnp.mean(xc * xc, axis=-1, keepdims=True)
    return (xc * lax.rsqrt(var + eps)).astype(x.dtype)


def alibi_slopes(n_heads):
    return 2.0 ** (-8.0 * jnp.arange(1, n_heads + 1, dtype=jnp.float32) / n_heads)


def chunked_decay_recurrence(q, k, v, log_decay, chunk):
    out_dtype = v.dtype
    b_, t_, h_, dk = q.shape
    dv = v.shape[-1]
    n = t_ // chunk

    def to_chunks(a):
        return a.astype(jnp.float32).reshape(b_, n, chunk, h_, a.shape[-1]).transpose(1, 0, 3, 2, 4)

    qc, kc, vc = to_chunks(q), to_chunks(k), to_chunks(v)
    bc = jnp.cumsum(to_chunks(log_decay), axis=3)
    causal = jnp.tril(jnp.ones((chunk, chunk), dtype=bool))
    per_channel = log_decay.shape[-1] != 1

    def step(state, inp):
        qi, ki, vi, bi = inp
        b_last = bi[:, :, -1:, :]
        if per_channel:
            diff = bi[:, :, :, None, :] - bi[:, :, None, :, :]
            w = jnp.exp(jnp.where(causal[:, :, None], diff, -jnp.inf))
            scores = jnp.einsum('bhtd,bhsd,bhtsd->bhts', qi, ki, w)
        else:
            diff = bi[:, :, :, None, 0] - bi[:, :, None, :, 0]
            w = jnp.exp(jnp.where(causal, diff, -jnp.inf))
            scores = jnp.einsum('bhtd,bhsd->bhts', qi, ki) * w
        o = (jnp.einsum('bhts,bhsv->bhtv', scores, vi)
             + jnp.einsum('bhtd,bhdv->bhtv', qi * jnp.exp(bi), state))
        k_to_end = ki * jnp.exp(b_last - bi)
        state = (jnp.exp(b_last[:, :, 0, :])[..., None] * state
                 + jnp.einsum('bhsd,bhsv->bhdv', k_to_end, vi))
        return state, o

    state0 = jnp.zeros((b_, h_, dk, dv), jnp.float32)
    _, o = lax.scan(step, state0, (qc, kc, vc, bc))
    o = o.transpose(1, 0, 3, 2, 4).reshape(b_, t_, h_, dv)
    return o.astype(out_dtype)


def hgrn2_mixer(q, f, i, g, lower_bound, norm_g):
    b_, t_, _ = q.shape
    shp = (b_, t_, HGRN_HEADS, HGRN_HEAD_DIM)
    forget = lower_bound + (1.0 - lower_bound) * jax.nn.sigmoid(f.astype(jnp.float32))
    o = chunked_decay_recurrence(jax.nn.silu(q).reshape(shp), (1.0 - forget).reshape(shp),
                                 i.reshape(shp), jnp.log(forget).reshape(shp), HGRN_CHUNK)
    o = rms_norm(o, norm_g.reshape(HGRN_HEADS, HGRN_HEAD_DIM)).reshape(b_, t_, HGRN_WIDTH)
    return o * jax.nn.silu(g)


def sliding_window_attention(q, k, v, sinks, slopes):
    b_, t_, _ = q.shape
    n = t_ // ATTN_BLOCK
    grp = ATTN_Q_HEADS // ATTN_KV_HEADS
    qb = q.astype(jnp.float32).reshape(b_, n, ATTN_BLOCK, ATTN_KV_HEADS, grp, ATTN_HEAD_DIM)

    def with_prev(a):
        a = a.astype(jnp.float32).reshape(b_, n, ATTN_BLOCK, ATTN_KV_HEADS, ATTN_HEAD_DIM)
        prev = jnp.concatenate([jnp.zeros_like(a[:, :1]), a[:, :-1]], axis=1)
        return jnp.concatenate([prev, a], axis=2)

    kk, vv = with_prev(k), with_prev(v)
    scores = jnp.einsum('bnqhgd,bnkhd->bnhgqk', qb, kk) * (ATTN_HEAD_DIM ** -0.5)
    dist = (jnp.arange(ATTN_BLOCK)[:, None] + ATTN_BLOCK) - jnp.arange(2 * ATTN_BLOCK)[None, :]
    in_window = (dist >= 0) & (dist < ATTN_WINDOW)
    key_pos = (jnp.arange(n)[:, None] * ATTN_BLOCK + jnp.arange(2 * ATTN_BLOCK)[None, :]
               - ATTN_BLOCK)
    valid = in_window[None] & (key_pos >= 0)[:, None, :]
    alibi = -slopes.astype(jnp.float32).reshape(ATTN_KV_HEADS, grp)[:, :, None, None] * dist.astype(jnp.float32)
    logits = jnp.where(valid[None, :, None, None], scores + alibi[None, None], -jnp.inf)
    sink = jnp.broadcast_to(sinks.astype(jnp.float32).reshape(ATTN_KV_HEADS, grp)[None, None, :, :, None, None],
                            logits.shape[:-1] + (1,))
    probs = jax.nn.softmax(jnp.concatenate([logits, sink], axis=-1), axis=-1)[..., :-1]
    out = jnp.einsum('bnhgqk,bnkhd->bnqhgd', probs, vv)
    return out.reshape(b_, t_, ATTN_Q_HEADS * ATTN_HEAD_DIM).astype(q.dtype)


def causal_depthwise_conv(x, w, b):
    ch = x.shape[-1]
    y = lax.conv_general_dilated(x, w[:, None, :].astype(x.dtype), window_strides=(1,),
                                 padding=[(SSD_CONV - 1, 0)],
                                 dimension_numbers=('NWC', 'WIO', 'NWC'),
                                 feature_group_count=ch)
    return y + b.astype(x.dtype)


def ssd_mixer(z, xbc, dt, conv_w, conv_b, dt_bias, a_log, d_skip, norm_g):
    b_, t_, _ = z.shape
    xbc = jax.nn.silu(causal_depthwise_conv(xbc, conv_w, conv_b))
    xs, bm, cm = split_columns(xbc, (SSD_WIDTH, SSD_GROUPS * SSD_STATE, SSD_GROUPS * SSD_STATE))
    dt = jax.nn.softplus(dt.astype(jnp.float32) + dt_bias.astype(jnp.float32))
    a = -jnp.exp(a_log.astype(jnp.float32))
    xs = xs.reshape(b_, t_, SSD_HEADS, SSD_HEAD_DIM)
    rep = SSD_HEADS // SSD_GROUPS
    bm = jnp.repeat(bm.reshape(b_, t_, SSD_GROUPS, SSD_STATE), rep, axis=2)
    cm = jnp.repeat(cm.reshape(b_, t_, SSD_GROUPS, SSD_STATE), rep, axis=2)
    y = chunked_decay_recurrence(cm, bm, xs * dt[..., None], (dt * a)[..., None], SSD_CHUNK)
    y = y + d_skip.astype(y.dtype)[:, None] * xs
    y = y.reshape(b_, t_, SSD_WIDTH)
    return rms_norm(y * jax.nn.silu(z), norm_g)


def retention_mixer(q, k, v, g):
    b_, t_, _ = q.shape
    log_gamma = jnp.log(1.0 - 2.0 ** (-5.0 - jnp.arange(RET_HEADS, dtype=jnp.float32)))
    log_decay = jnp.broadcast_to(log_gamma[:, None], (b_, t_, RET_HEADS, 1))
    o = chunked_decay_recurrence(q.reshape(b_, t_, RET_HEADS, RET_QK_DIM),
                                 k.reshape(b_, t_, RET_HEADS, RET_QK_DIM) * (RET_QK_DIM ** -0.5),
                                 v.reshape(b_, t_, RET_HEADS, RET_V_DIM), log_decay, RET_CHUNK)
    o = head_group_norm(o).reshape(b_, t_, RET_HEADS * RET_V_DIM)
    return o * jax.nn.silu(g)


def setup_inputs(seed: int = 0) -> dict:
    key = jax.random.key(seed)
    ks = jax.random.split(key, 22)
    f32 = jnp.float32
    L, D, F = DEPTH, D_MODEL, FFN_HIDDEN

    def normal(k, shape, scale):
        return jax.random.normal(k, shape, f32) * scale

    def gain(k, shape):
        return 1.0 + 0.05 * jax.random.normal(k, shape, f32)

    dt0 = jnp.exp(jax.random.uniform(ks[10], (L, SSD_HEADS), f32, math.log(1e-3), math.log(1e-1)))
    return {
        'x': normal(ks[0], (BATCH, SEQ, D), 1.0),
        'norm_mix_pre': gain(ks[1], (L, D)),
        'norm_mix_post': gain(ks[2], (L, D)),
        'norm_ffn_pre': gain(ks[3], (L, D)),
        'norm_ffn_post': gain(ks[4], (L, D)),
        'w_in': normal(ks[5], (L, D, IN_WIDTH), D ** -0.5),
        'hgrn_lb_logits': normal(ks[6], (L, HGRN_WIDTH), 0.3),
        'hgrn_norm': gain(ks[7], (L, HGRN_WIDTH)),
        'attn_sinks': normal(ks[8], (L, ATTN_Q_HEADS), 0.5),
        'ssd_conv_w': normal(ks[9], (L, SSD_CONV, SSD_XBC_WIDTH), SSD_CONV ** -0.5),
        'ssd_conv_b': normal(ks[11], (L, SSD_XBC_WIDTH), 0.02),
        'ssd_dt_bias': dt0 + jnp.log(-jnp.expm1(-dt0)),
        'ssd_a_log': jnp.log(jax.random.uniform(ks[12], (L, SSD_HEADS), f32, 1.0, 16.0)),
        'ssd_d': 1.0 + 0.1 * jax.random.normal(ks[13], (L, SSD_HEADS), f32),
        'ssd_norm': gain(ks[14], (L, SSD_WIDTH)),
        'w_gate_up': normal(ks[15], (L, GATE_RANK, N_BRANCH * D), GATE_RANK ** -0.5),
        'b_gate': normal(ks[16], (L, N_BRANCH * D), 0.1),
        'w_branch': normal(ks[17], (L, N_BRANCH, BRANCH_WIDTH, D), BRANCH_WIDTH ** -0.5),
        'w_out': normal(ks[18], (L, D, D), D ** -0.5),
        'w_ffn_gate': normal(ks[19], (L, D, F), D ** -0.5),
        'w_ffn_up': normal(ks[20], (L, D, F), D ** -0.5),
        'w_ffn_down': normal(ks[21], (L, F, D), F ** -0.5),
    }


def reference(x, norm_mix_pre, norm_mix_post, norm_ffn_pre, norm_ffn_post, w_in,
              hgrn_lb_logits, hgrn_norm, attn_sinks, ssd_conv_w, ssd_conv_b, ssd_dt_bias,
              ssd_a_log, ssd_d, ssd_norm, w_gate_up, b_gate, w_branch, w_out,
              w_ffn_gate, w_ffn_up, w_ffn_down):
    b_, t_, d_ = x.shape
    p = jax.nn.softmax(hgrn_lb_logits.astype(jnp.float32), axis=0)
    lower_bounds = jnp.cumsum(p, axis=0) - p[0]
    slopes = alibi_slopes(ATTN_Q_HEADS)
    for l in range(DEPTH):
        h = rms_norm(x, norm_mix_pre[l])
        proj = h @ w_in[l]
        (hq, hf, hi, hg, aq, ak, av, sz, sxbc, sdt,
         rq, rk, rv, rg, gate_code) = split_columns(proj, SPLIT_SIZES)
        outs = (
            hgrn2_mixer(hq, hf, hi, hg, lower_bounds[l], hgrn_norm[l]),
            sliding_window_attention(aq, ak, av, attn_sinks[l], slopes),
            ssd_mixer(sz, sxbc, sdt, ssd_conv_w[l], ssd_conv_b[l], ssd_dt_bias[l],
                      ssd_a_log[l], ssd_d[l], ssd_norm[l]),
            retention_mixer(rq, rk, rv, rg),
        )
        gates = jax.nn.sigmoid((gate_code @ w_gate_up[l] + b_gate[l]).astype(jnp.float32))
        gates = gates.reshape(b_, t_, N_BRANCH, d_).astype(x.dtype)
        merged = gates[:, :, 0, :] * (outs[0].astype(x.dtype) @ w_branch[l, 0])
        for br in range(1, N_BRANCH):
            merged = merged + gates[:, :, br, :] * (outs[br].astype(x.dtype) @ w_branch[l, br])
        x = x + rms_norm(merged @ w_out[l], norm_mix_post[l])
        h = rms_norm(x, norm_ffn_pre[l])
        ff = (jax.nn.silu(h @ w_ffn_gate[l]) * (h @ w_ffn_up[l])) @ w_ffn_down[l]
        x = x + rms_norm(ff, norm_ffn_post[l])
    return x
```

```python
import functools
import math

import numpy as np
import jax
import jax.numpy as jnp
from jax import lax
from jax.experimental import pallas as pl
from jax.experimental.pallas import tpu as pltpu

F32 = jnp.float32
BF16 = jnp.bfloat16
HIGHEST = lax.Precision.HIGHEST

D_MODEL = 4096
DEPTH = 4
BRANCH_WIDTH = D_MODEL // 4
HGRN_HEADS, HGRN_DIM, HGRN_CHUNK = 8, 128, 64
ATTN_Q_HEADS, ATTN_KV_HEADS, ATTN_DIM, ATTN_BLOCK = 16, 4, 64, 128
SSD_HEADS, SSD_DIM, SSD_GROUPS, SSD_STATE, SSD_CONV, SSD_CHUNK = 16, 64, 4, 128, 4, 128
RET_HEADS, RET_QK, RET_V, RET_CHUNK = 8, 64, 128, 128
GATE_RANK = 256
FFN_HIDDEN = -(-8 * D_MODEL // (3 * 256)) * 256
FFN_PAD = -(-FFN_HIDDEN // 1024) * 1024
IN_WIDTH = 12048

LANES = 128
VMEM_LIMIT = 56 * 1024 * 1024

SEGMENTS = (
    ("sxbc", 6656, 2048),
    ("hq", 0, 1024), ("hf", 1024, 1024), ("hi", 2048, 1024), ("hg", 3072, 1024),
    ("aq", 4096, 1024), ("sz", 5632, 1024), ("rv", 9744, 1024), ("rg", 10768, 1024),
    ("rq", 8720, 512), ("rk", 9232, 512),
    ("ak", 5120, 256), ("av", 5376, 256), ("gc", 11792, 256),
    ("dt", 8704, 16),
)
PROJ_WIDTH = 12288
COL = {}
_off = 0
for _name, _, _w in SEGMENTS:
    COL[_name] = _off
    _off += _w
assert _off == IN_WIDTH and all(COL[n] % max(w, LANES) == 0 for n, _, w in SEGMENTS)

NT_DIMS = (((1,), (1,)), ((), ()))
TN_DIMS = (((0,), (0,)), ((), ()))
NEG_BIG = -1e30


def _sigmoid(x):
    return 1.0 / (1.0 + jnp.exp(-x))


def _silu(x):
    return x * _sigmoid(x)


def _rms(x, eps):
    return x * lax.rsqrt(jnp.mean(x * x, axis=-1, keepdims=True) + eps)


def _dot(a, b, dims=None, precision=None):
    if dims is None:
        dims = (((a.ndim - 1,), (0,)), ((), ()))
    return lax.dot_general(a, b, dims, precision=precision, preferred_element_type=F32)


def _cumsum_rows(x):
    n = x.shape[0]
    row = lax.broadcasted_iota(jnp.int32, x.shape, 0)
    k = 1
    while k < n:
        x = x + jnp.where(row >= k, pltpu.roll(x, k, 0), 0.0)
        k *= 2
    return x


def _params(*sem):
    return pltpu.CompilerParams(dimension_semantics=sem, vmem_limit_bytes=VMEM_LIMIT)


def _matmul_kernel(a_ref, b_ref, o_ref, acc_ref):
    k = pl.program_id(2)

    @pl.when(k == 0)
    def _():
        acc_ref[...] = jnp.zeros_like(acc_ref)

    acc_ref[...] += _dot(a_ref[...], b_ref[...])

    @pl.when(k == pl.num_programs(2) - 1)
    def _():
        o_ref[...] = acc_ref[...].astype(o_ref.dtype)


def _matmul(a, b, out_dtype, tm=1024, tn=1024, tk=1024):
    m, kdim = a.shape
    _, n = b.shape
    return pl.pallas_call(
        _matmul_kernel,
        grid=(m // tm, n // tn, kdim // tk),
        in_specs=[pl.BlockSpec((tm, tk), lambda i, j, k: (i, k)),
                  pl.BlockSpec((tk, tn), lambda i, j, k: (k, j))],
        out_specs=pl.BlockSpec((tm, tn), lambda i, j, k: (i, j)),
        out_shape=jax.ShapeDtypeStruct((m, n), out_dtype),
        scratch_shapes=[pltpu.VMEM((tm, tn), F32)],
        compiler_params=_params("parallel", "parallel", "arbitrary"),
        name="matmul",
    )(a, b)


def _norm_kernel(x_ref, g_ref, h_ref):
    h_ref[...] = (_rms(x_ref[...], 1e-6) * g_ref[...]).astype(BF16)


def _norm(x, g, rows=256):
    t, d = x.shape
    return pl.pallas_call(
        _norm_kernel,
        grid=(t // rows,),
        in_specs=[pl.BlockSpec((rows, d), lambda i: (i, 0)),
                  pl.BlockSpec((1, d), lambda i: (0, 0))],
        out_specs=pl.BlockSpec((rows, d), lambda i: (i, 0)),
        out_shape=jax.ShapeDtypeStruct((t, d), BF16),
        compiler_params=_params("parallel"),
        name="norm",
    )(x, g.reshape(1, d))


def _resid_norm_kernel(x_ref, y_ref, gpost_ref, gnext_ref, xo_ref, h_ref):
    xn = x_ref[...] + _rms(y_ref[...], 1e-6) * gpost_ref[...]
    xo_ref[...] = xn
    h_ref[...] = (_rms(xn, 1e-6) * gnext_ref[...]).astype(BF16)


def _resid_kernel(x_ref, y_ref, gpost_ref, xo_ref):
    xo_ref[...] = x_ref[...] + _rms(y_ref[...], 1e-6) * gpost_ref[...]


def _resid_norm(x, y, g_post, g_next, rows=256):
    t, d = x.shape
    row_spec = pl.BlockSpec((rows, d), lambda i: (i, 0))
    vec_spec = pl.BlockSpec((1, d), lambda i: (0, 0))
    if g_next is None:
        return pl.pallas_call(
            _resid_kernel, grid=(t // rows,),
            in_specs=[row_spec, row_spec, vec_spec],
            out_specs=row_spec,
            out_shape=jax.ShapeDtypeStruct((t, d), F32),
            input_output_aliases={0: 0},
            compiler_params=_params("parallel"),
            name="resid",
        )(x, y, g_post.reshape(1, d)), None
    return pl.pallas_call(
        _resid_norm_kernel, grid=(t // rows,),
        in_specs=[row_spec, row_spec, vec_spec, vec_spec],
        out_specs=[row_spec, row_spec],
        out_shape=[jax.ShapeDtypeStruct((t, d), F32), jax.ShapeDtypeStruct((t, d), BF16)],
        input_output_aliases={0: 0},
        compiler_params=_params("parallel"),
        name="resid_norm",
    )(x, y, g_post.reshape(1, d), g_next.reshape(1, d))


def _merge_kernel(oa_ref, ob_ref, oc_ref, od_ref, gc_ref, wb_ref,
                  wg0_ref, wg1_ref, wg2_ref, wg3_ref, bg0_ref, bg1_ref, bg2_ref, bg3_ref,
                  out_ref):
    gc = gc_ref[...].astype(BF16)
    branches = ((oa_ref, wg0_ref, bg0_ref), (ob_ref, wg1_ref, bg1_ref),
                (oc_ref, wg2_ref, bg2_ref), (od_ref, wg3_ref, bg3_ref))
    acc = None
    for br, (o_ref, wg_ref, bg_ref) in enumerate(branches):
        y = _dot(o_ref[...], wb_ref[br])
        gate = _sigmoid(_dot(gc, wg_ref[...]) + bg_ref[...])
        acc = gate * y if acc is None else acc + gate * y
    out_ref[...] = acc.astype(BF16)


def _merge(outs, proj, w_branch, w_gate_up, b_gate, tm=1024, tn=512):
    t = proj.shape[0]
    d = D_MODEL
    nj = d // tn
    o_spec = pl.BlockSpec((tm, BRANCH_WIDTH), lambda i, j: (i, 0))
    gc_spec = pl.BlockSpec((tm, GATE_RANK), lambda i, j: (i, COL["gc"] // GATE_RANK))
    wb_spec = pl.BlockSpec((4, BRANCH_WIDTH, tn), lambda i, j: (0, 0, j))
    wg_specs = [pl.BlockSpec((GATE_RANK, tn), lambda i, j, br=br: (0, br * nj + j)) for br in range(4)]
    bg_specs = [pl.BlockSpec((1, tn), lambda i, j, br=br: (0, br * nj + j)) for br in range(4)]
    bg = b_gate.reshape(1, 4 * d)
    return pl.pallas_call(
        _merge_kernel,
        grid=(t // tm, nj),
        in_specs=[o_spec] * 4 + [gc_spec, wb_spec] + wg_specs + bg_specs,
        out_specs=pl.BlockSpec((tm, tn), lambda i, j: (i, j)),
        out_shape=jax.ShapeDtypeStruct((t, d), BF16),
        compiler_params=_params("parallel", "parallel"),
        name="merge",
    )(*outs, proj, w_branch, *([w_gate_up] * 4), *([bg] * 4))


def _gate_up_kernel(h_ref, wg_ref, wu_ref, o_ref, accg_ref, accu_ref):
    k = pl.program_id(2)

    @pl.when(k == 0)
    def _():
        accg_ref[...] = jnp.zeros_like(accg_ref)
        accu_ref[...] = jnp.zeros_like(accu_ref)

    h = h_ref[...]
    accg_ref[...] += _dot(h, wg_ref[...])
    accu_ref[...] += _dot(h, wu_ref[...])

    @pl.when(k == pl.num_programs(2) - 1)
    def _():
        o_ref[...] = (_silu(accg_ref[...]) * accu_ref[...]).astype(BF16)


def _gate_up(h, wg, wu, tm=1024, tn=1024, tk=1024):
    t, d = h.shape
    f = wg.shape[1]
    w_spec = pl.BlockSpec((tk, tn), lambda i, j, k: (k, j))
    return pl.pallas_call(
        _gate_up_kernel,
        grid=(t // tm, f // tn, d // tk),
        in_specs=[pl.BlockSpec((tm, tk), lambda i, j, k: (i, k)), w_spec, w_spec],
        out_specs=pl.BlockSpec((tm, tn), lambda i, j, k: (i, j)),
        out_shape=jax.ShapeDtypeStruct((t, f), BF16),
        scratch_shapes=[pltpu.VMEM((tm, tn), F32), pltpu.VMEM((tm, tn), F32)],
        compiler_params=_params("parallel", "parallel", "arbitrary"),
        name="ffn_gate_up",
    )(h, wg, wu)


def _lower_bound_kernel(l_ref, o_ref):
    lg = l_ref[...]
    e = jnp.exp(lg - jnp.max(lg, axis=0, keepdims=True))
    p = e / jnp.sum(e, axis=0, keepdims=True)
    run = p[0:1]
    o_ref[0:1, :] = run - p[0:1]
    for l in range(1, DEPTH):
        run = run + p[l:l + 1]
        o_ref[l:l + 1, :] = run - p[0:1]


def _lower_bounds(logits):
    return pl.pallas_call(
        _lower_bound_kernel,
        out_shape=jax.ShapeDtypeStruct(logits.shape, F32),
        name="hgrn_lower_bounds",
    )(logits)


HGRN_LEVELS = (32, 16, 8, 4, 2, 1)


def _hgrn_exponent_matrix():
    c = HGRN_CHUNK
    mat = np.zeros(((len(HGRN_LEVELS) + 2) * c, c), np.float32)
    for li, m in enumerate(HGRN_LEVELS):
        for t in range(c):
            r = (t // (2 * m)) * 2 * m + m - 1
            if (t // m) % 2 == 1:
                mat[li * c + t, r + 1:t + 1] = 1.0
            else:
                mat[li * c + t, t + 1:r + 1] = 1.0
    for t in range(c):
        mat[len(HGRN_LEVELS) * c + t, :t + 1] = 1.0
        mat[(len(HGRN_LEVELS) + 1) * c + t, t + 1:] = 1.0
    return mat


def _hgrn_kernel(q_ref, f_ref, i_ref, g_ref, lb_ref, ng_ref, mat_ref, o_ref, state_ref):
    c = HGRN_CHUNK
    nl = len(HGRN_LEVELS)

    @pl.when(pl.program_id(0) == 0)
    def _():
        state_ref[...] = jnp.zeros_like(state_ref)

    lb = lb_ref[...]
    forget = lb + (1.0 - lb) * _sigmoid(f_ref[...])
    key = 1.0 - forget
    query = _silu(q_ref[...])
    decay = jnp.exp(_dot(mat_ref[...], jnp.log(forget), precision=HIGHEST))

    row = lax.broadcasted_iota(jnp.int32, (c, c), 0)
    col = lax.broadcasted_iota(jnp.int32, (c, c), 1)
    row1 = lax.broadcasted_iota(jnp.int32, (c, 1), 0)
    shifts = [m.bit_length() - 1 for m in HGRN_LEVELS]
    upper = [((row1 >> s) & 1) == 1 for s in shifts]
    same_block = [(row >> (s + 1)) == (col >> (s + 1)) for s in shifts]

    for h in range(HGRN_HEADS):
        sl = slice(h * HGRN_DIM, (h + 1) * HGRN_DIM)
        qh, kh = query[:, sl], key[:, sl]
        vh = i_ref[:, sl].astype(BF16)
        scores = jnp.where(row == col, _dot(qh.astype(BF16), kh.astype(BF16), NT_DIMS), 0.0)
        for li in range(nl):
            dl = decay[li * c:(li + 1) * c, sl]
            qt = jnp.where(upper[li], qh * dl, 0.0).astype(BF16)
            kt = jnp.where(upper[li], 0.0, kh * dl).astype(BF16)
            scores = scores + jnp.where(same_block[li], _dot(qt, kt, NT_DIMS), 0.0)
        q_in = (qh * decay[nl * c:(nl + 1) * c, sl]).astype(BF16)
        k_out = (kh * decay[(nl + 1) * c:(nl + 2) * c, sl]).astype(BF16)
        state_t = state_ref[h]
        o = _dot(scores.astype(BF16), vh) + _dot(q_in, state_t.astype(BF16), NT_DIMS)
        chunk_decay = decay[(nl + 1) * c - 1:(nl + 1) * c, sl]
        state_ref[h] = state_t * chunk_decay + _dot(vh, k_out, TN_DIMS)
        gh = g_ref[:, sl]
        o_ref[:, sl] = (_rms(o, 1e-6) * ng_ref[:, sl] * _silu(gh)).astype(BF16)


def _hgrn(proj, lower_bound, norm_g):
    t = proj.shape[0]
    c = HGRN_CHUNK
    w = HGRN_HEADS * HGRN_DIM
    mat = jnp.asarray(_hgrn_exponent_matrix())

    def seg(name):
        return pl.BlockSpec((c, w), lambda i, b=COL[name] // w: (i, b))

    vec = pl.BlockSpec((1, w), lambda i: (0, 0))
    return pl.pallas_call(
        _hgrn_kernel,
        grid=(t // c,),
        in_specs=[seg("hq"), seg("hf"), seg("hi"), seg("hg"), vec, vec,
                  pl.BlockSpec(mat.shape, lambda i: (0, 0))],
        out_specs=pl.BlockSpec((c, w), lambda i: (i, 0)),
        out_shape=jax.ShapeDtypeStruct((t, w), BF16),
        scratch_shapes=[pltpu.VMEM((HGRN_HEADS, HGRN_DIM, HGRN_DIM), F32)],
        compiler_params=_params("arbitrary"),
        name="hgrn2",
    )(proj, proj, proj, proj, lower_bound.reshape(1, w), norm_g.reshape(1, w), mat)


def _attn_kernel(sink_ref, q_ref, k_ref, v_ref, o_ref, prev_ref):
    c = ATTN_BLOCK
    kvw = ATTN_KV_HEADS * ATTN_DIM
    first = pl.program_id(0) == 0

    @pl.when(first)
    def _():
        prev_ref[...] = jnp.zeros_like(prev_ref)

    qi = lax.broadcasted_iota(jnp.int32, (c, 2 * c), 0)
    kj = lax.broadcasted_iota(jnp.int32, (c, 2 * c), 1)
    dist = qi + c - kj
    valid = (dist >= 0) & (dist < c) & ((kj >= c) | jnp.logical_not(first))
    dist_f = dist.astype(F32)

    k_all = jnp.concatenate([prev_ref[:, :kvw], k_ref[...]], axis=0).astype(BF16)
    v_all = jnp.concatenate([prev_ref[:, kvw:], v_ref[...]], axis=0).astype(BF16)
    group = ATTN_Q_HEADS // ATTN_KV_HEADS
    outs = []
    for h in range(ATTN_Q_HEADS):
        g = h // group
        slope = 2.0 ** (-8.0 * (h + 1) / ATTN_Q_HEADS)
        qh = q_ref[:, h * ATTN_DIM:(h + 1) * ATTN_DIM].astype(BF16)
        kg = k_all[:, g * ATTN_DIM:(g + 1) * ATTN_DIM]
        vg = v_all[:, g * ATTN_DIM:(g + 1) * ATTN_DIM]
        logits = _dot(qh, kg, NT_DIMS) * (ATTN_DIM ** -0.5) - slope * dist_f
        logits = jnp.where(valid, logits, NEG_BIG)
        sink = sink_ref[h]
        mx = jnp.maximum(jnp.max(logits, axis=-1, keepdims=True), sink)
        p = jnp.exp(logits - mx)
        denom = jnp.sum(p, axis=-1, keepdims=True) + jnp.exp(sink - mx)
        outs.append(_dot(p.astype(BF16), vg) / denom)
    o_ref[...] = jnp.concatenate(outs, axis=1).astype(BF16)
    prev_ref[:, :kvw] = k_ref[...]
    prev_ref[:, kvw:] = v_ref[...]


def _attn(proj, sinks):
    t = proj.shape[0]
    c = ATTN_BLOCK
    qw = ATTN_Q_HEADS * ATTN_DIM
    kvw = ATTN_KV_HEADS * ATTN_DIM
    return pl.pallas_call(
        _attn_kernel,
        grid=(t // c,),
        in_specs=[pl.BlockSpec(memory_space=pltpu.SMEM),
                  pl.BlockSpec((c, qw), lambda i: (i, COL["aq"] // qw)),
                  pl.BlockSpec((c, kvw), lambda i: (i, COL["ak"] // kvw)),
                  pl.BlockSpec((c, kvw), lambda i: (i, COL["av"] // kvw))],
        out_specs=pl.BlockSpec((c, qw), lambda i: (i, 0)),
        out_shape=jax.ShapeDtypeStruct((t, qw), BF16),
        scratch_shapes=[pltpu.VMEM((c, 2 * kvw), F32)],
        compiler_params=_params("arbitrary"),
        name="swa",
    )(sinks, proj, proj, proj)


def _ssd_kernel(xbc_ref, z_ref, dt_ref, cw_ref, cb_ref, dtb_ref, alog_ref, dskip_ref, ng_ref,
                expand_ref, o_ref, prev_ref, state_ref):
    c = SSD_CHUNK
    width = SSD_HEADS * SSD_DIM
    gw = SSD_GROUPS * SSD_STATE
    per_group = SSD_HEADS // SSD_GROUPS

    @pl.when(pl.program_id(0) == 0)
    def _():
        prev_ref[...] = jnp.zeros_like(prev_ref)
        state_ref[...] = jnp.zeros_like(state_ref)

    x = xbc_ref[...]
    prev = prev_ref[...]
    row = lax.broadcasted_iota(jnp.int32, x.shape, 0)
    conv = x * cw_ref[SSD_CONV - 1:SSD_CONV, :] + cb_ref[...]
    for back in range(1, SSD_CONV):
        shifted = jnp.where(row >= back, pltpu.roll(x, back, 0), pltpu.roll(prev, back, 0))
        conv = conv + shifted * cw_ref[SSD_CONV - 1 - back:SSD_CONV - back, :]
    prev_ref[...] = x
    xbc = _silu(conv)
    xs = xbc[:, :width]
    bm = xbc[:, width:width + gw].astype(BF16)
    cm = xbc[:, width + gw:].astype(BF16)

    dt_in = dt_ref[...] + dtb_ref[...]
    dt = jnp.maximum(dt_in, 0.0) + jnp.log1p(jnp.exp(-jnp.abs(dt_in)))
    b = _cumsum_rows(dt * (-jnp.exp(alog_ref[...])))
    b_t = b.T
    wide = _dot(jnp.concatenate([dt, b], axis=0), expand_ref[...], precision=HIGHEST)
    dt_w, b_w = wide[:c], wide[c:]
    b_last = b_w[c - 1:c, :]
    v = xs * dt_w
    v_bf = v.astype(BF16)
    v_out = (v * jnp.exp(b_last - b_w)).astype(BF16)
    in_scale = jnp.exp(b_w)

    ti = lax.broadcasted_iota(jnp.int32, (c, c), 0)
    si = lax.broadcasted_iota(jnp.int32, (c, c), 1)
    causal = ti >= si
    lane = lax.broadcasted_iota(jnp.int32, (c, 2 * SSD_DIM), 1)
    low = lane < SSD_DIM

    pieces = []
    for g in range(SSD_GROUPS):
        cg = cm[:, g * SSD_STATE:(g + 1) * SSD_STATE]
        bg = bm[:, g * SSD_STATE:(g + 1) * SSD_STATE]
        cb = _dot(cg, bg, NT_DIMS)
        gs = slice(g * per_group * SSD_DIM, (g + 1) * per_group * SSD_DIM)
        state = state_ref[g]
        inter = _dot(cg, state.astype(BF16)) * in_scale[:, gs]
        state_ref[g] = state * jnp.exp(b_last[:, gs]) + _dot(bg, v_out[:, gs], TN_DIMS)
        for pair in range(per_group // 2):
            p_idx = g * (per_group // 2) + pair
            probs = []
            for h in (2 * p_idx, 2 * p_idx + 1):
                w = jnp.exp(jnp.where(causal, b[:, h:h + 1] - b_t[h:h + 1, :], NEG_BIG))
                probs.append((cb * w).astype(BF16))
            v_pair = v_bf[:, p_idx * 2 * SSD_DIM:(p_idx + 1) * 2 * SSD_DIM]
            v_diag = jnp.concatenate([jnp.where(low, v_pair, jnp.zeros_like(v_pair)),
                                      jnp.where(low, jnp.zeros_like(v_pair), v_pair)], axis=0)
            intra = _dot(jnp.concatenate(probs, axis=1), v_diag)
            pieces.append(intra + inter[:, pair * 2 * SSD_DIM:(pair + 1) * 2 * SSD_DIM])
    y = jnp.concatenate(pieces, axis=1) + dskip_ref[...] * xs
    y = y * _silu(z_ref[...])
    o_ref[...] = (_rms(y, 1e-6) * ng_ref[...]).astype(BF16)


def _ssd(proj, conv_w, conv_b, dt_bias, a_log, d_skip, norm_g):
    t = proj.shape[0]
    c = SSD_CHUNK
    width = SSD_HEADS * SSD_DIM
    xw = width + 2 * SSD_GROUPS * SSD_STATE

    def pad_lanes(a):
        return jnp.pad(a.reshape(1, -1), ((0, 0), (0, LANES - a.shape[-1])))

    expand = np.zeros((LANES, width), np.float32)
    for h in range(SSD_HEADS):
        expand[h, h * SSD_DIM:(h + 1) * SSD_DIM] = 1.0
    full = lambda shape: pl.BlockSpec(shape, lambda i: (0, 0))
    return pl.pallas_call(
        _ssd_kernel,
        grid=(t // c,),
        in_specs=[pl.BlockSpec((c, xw), lambda i: (i, COL["sxbc"] // xw)),
                  pl.BlockSpec((c, width), lambda i: (i, COL["sz"] // width)),
                  pl.BlockSpec((c, LANES), lambda i: (i, COL["dt"] // LANES)),
                  full((SSD_CONV, xw)), full((1, xw)), full((1, LANES)), full((1, LANES)),
                  full((1, width)), full((1, width)), full((LANES, width))],
        out_specs=pl.BlockSpec((c, width), lambda i: (i, 0)),
        out_shape=jax.ShapeDtypeStruct((t, width), BF16),
        scratch_shapes=[pltpu.VMEM((c, xw), F32),
                        pltpu.VMEM((SSD_GROUPS, SSD_STATE, width // SSD_GROUPS), F32)],
        compiler_params=_params("arbitrary"),
        name="ssd",
    )(proj, proj, proj, conv_w, conv_b.reshape(1, xw), pad_lanes(dt_bias), pad_lanes(a_log),
      jnp.repeat(d_skip, SSD_DIM).reshape(1, width), norm_g.reshape(1, width), jnp.asarray(expand))


def _ret_log_gamma(h):
    return math.log(1.0 - 2.0 ** (-5.0 - h))


def _ret_kernel(q_ref, k_ref, v_ref, g_ref, o_ref, state_ref, mask_ref):
    c = RET_CHUNK

    @pl.when(pl.program_id(0) == 0)
    def _():
        state_ref[...] = jnp.zeros_like(state_ref)
        ti = lax.broadcasted_iota(jnp.int32, (c, c), 0)
        si = lax.broadcasted_iota(jnp.int32, (c, c), 1)
        lag = (ti - si).astype(F32)
        for h in range(RET_HEADS):
            mask_ref[h] = jnp.where(ti >= si, jnp.exp(lag * _ret_log_gamma(h)), 0.0)

    pos = lax.broadcasted_iota(jnp.int32, (c, 1), 0).astype(F32)
    for h in range(RET_HEADS):
        lg = _ret_log_gamma(h)
        qh = q_ref[:, h * RET_QK:(h + 1) * RET_QK]
        kh = k_ref[:, h * RET_QK:(h + 1) * RET_QK] * (RET_QK ** -0.5)
        vh = v_ref[:, h * RET_V:(h + 1) * RET_V].astype(BF16)
        scores = _dot(qh.astype(BF16), kh.astype(BF16), NT_DIMS) * mask_ref[h]
        q_in = (qh * jnp.exp((pos + 1.0) * lg)).astype(BF16)
        k_out = (kh * jnp.exp((c - 1.0 - pos) * lg)).astype(BF16)
        state_t = state_ref[h]
        o = _dot(scores.astype(BF16), vh) + _dot(q_in, state_t.astype(BF16), NT_DIMS)
        state_ref[h] = state_t * math.exp(c * lg) + _dot(vh, k_out, TN_DIMS)
        centered = o - jnp.mean(o, axis=-1, keepdims=True)
        normed = centered * lax.rsqrt(jnp.mean(centered * centered, axis=-1, keepdims=True) + 1e-5)
        gh = g_ref[:, h * RET_V:(h + 1) * RET_V]
        o_ref[:, h * RET_V:(h + 1) * RET_V] = (normed * _silu(gh)).astype(BF16)


def _retention(proj):
    t = proj.shape[0]
    c = RET_CHUNK
    qkw = RET_HEADS * RET_QK
    vw = RET_HEADS * RET_V
    return pl.pallas_call(
        _ret_kernel,
        grid=(t // c,),
        in_specs=[pl.BlockSpec((c, qkw), lambda i: (i, COL["rq"] // qkw)),
                  pl.BlockSpec((c, qkw), lambda i: (i, COL["rk"] // qkw)),
                  pl.BlockSpec((c, vw), lambda i: (i, COL["rv"] // vw)),
                  pl.BlockSpec((c, vw), lambda i: (i, COL["rg"] // vw))],
        out_specs=pl.BlockSpec((c, vw), lambda i: (i, 0)),
        out_shape=jax.ShapeDtypeStruct((t, vw), BF16),
        scratch_shapes=[pltpu.VMEM((RET_HEADS, RET_V, RET_QK), F32),
                        pltpu.VMEM((RET_HEADS, c, c), F32)],
        compiler_params=_params("arbitrary"),
        name="retention",
    )(proj, proj, proj, proj)


def _prep_w_in(w):
    parts = [w[:, off:off + width].astype(BF16) for _, off, width in SEGMENTS]
    parts.append(jnp.zeros((w.shape[0], PROJ_WIDTH - IN_WIDTH), BF16))
    return jnp.concatenate(parts, axis=1)


def _pad_cols(w, n):
    return jnp.pad(w.astype(BF16), ((0, 0), (0, n - w.shape[1])))


def _pad_rows(w, n):
    return jnp.pad(w.astype(BF16), ((0, n - w.shape[0]), (0, 0)))


def kernel(x, norm_mix_pre, norm_mix_post, norm_ffn_pre, norm_ffn_post, w_in,
           hgrn_lb_logits, hgrn_norm, attn_sinks, ssd_conv_w, ssd_conv_b, ssd_dt_bias,
           ssd_a_log, ssd_d, ssd_norm, w_gate_up, b_gate, w_branch, w_out,
           w_ffn_gate, w_ffn_up, w_ffn_down):
    b_, t_, d_ = x.shape
    xr = x.reshape(b_ * t_, d_)
    lower_bounds = _lower_bounds(hgrn_lb_logits)
    h = _norm(xr, norm_mix_pre[0])
    for l in range(DEPTH):
        proj = _matmul(h, _prep_w_in(w_in[l]), F32)
        outs = (
            _hgrn(proj, lower_bounds[l], hgrn_norm[l]),
            _attn(proj, attn_sinks[l]),
            _ssd(proj, ssd_conv_w[l], ssd_conv_b[l], ssd_dt_bias[l], ssd_a_log[l], ssd_d[l],
                 ssd_norm[l]),
            _retention(proj),
        )
        merged = _merge(outs, proj, w_branch[l].astype(BF16), w_gate_up[l].astype(BF16), b_gate[l])
        y = _matmul(merged, w_out[l].astype(BF16), F32)
        xr, h = _resid_norm(xr, y, norm_mix_post[l], norm_ffn_pre[l])
        u = _gate_up(h, _pad_cols(w_ffn_gate[l], FFN_PAD), _pad_cols(w_ffn_up[l], FFN_PAD))
        ff = _matmul(u, _pad_rows(w_ffn_down[l], FFN_PAD), F32)
        g_next = norm_mix_pre[l + 1] if l + 1 < DEPTH else None
        xr, h = _resid_norm(xr, ff, norm_ffn_post[l], g_next)
    return xr.reshape(b_, t_, d_)
```

```python
import functools
import math

import numpy as np
import jax
import jax.numpy as jnp
from jax import lax
from jax.experimental import pallas as pl
from jax.experimental.pallas import tpu as pltpu

F32 = jnp.float32
BF16 = jnp.bfloat16
HIGHEST = lax.Precision.HIGHEST

D_MODEL = 4096
DEPTH = 4
BRANCH_WIDTH = D_MODEL // 4
HGRN_HEADS, HGRN_DIM, HGRN_CHUNK = 8, 128, 64
ATTN_Q_HEADS, ATTN_KV_HEADS, ATTN_DIM, ATTN_BLOCK = 16, 4, 64, 128
SSD_HEADS, SSD_DIM, SSD_GROUPS, SSD_STATE, SSD_CONV, SSD_CHUNK = 16, 64, 4, 128, 4, 128
RET_HEADS, RET_QK, RET_V, RET_CHUNK = 8, 64, 128, 128
GATE_RANK = 256
FFN_HIDDEN = -(-8 * D_MODEL // (3 * 256)) * 256
FFN_BLOCK = 256
FFN_KBLOCK = 1024
FFN_PAD = -(-FFN_HIDDEN // FFN_KBLOCK) * FFN_KBLOCK
IN_WIDTH = 12048

LANES = 128
VMEM_LIMIT = 56 * 1024 * 1024

IN_BLOCK = 256
MAIN_WIDTH = 8704
MAIN_COL = {"hq": 0, "hf": 1024, "hi": 2048, "hg": 3072, "aq": 4096, "sz": 5120, "xs": 6144,
            "ak": 7168, "av": 7424, "sb": 7680, "sc": 8192}
DT_SRC = 8704
TAIL_SRC = 8720
TAIL_COL = {"rq": 0, "rk": 512, "rv": 1024, "rg": 2048, "gc": 3072, "dt": 3328}
TAIL_WIDTH = 3584


def _main_block_perm(j):
    return jnp.where(j < 20, j, jnp.where(j < 22, j + 8, jnp.where(j < 30, j - 2, j)))


NT_DIMS = (((1,), (1,)), ((), ()))
TN_DIMS = (((0,), (0,)), ((), ()))
NEG_BIG = -1e30


def _sigmoid(x):
    return 1.0 / (1.0 + jnp.exp(-x))


def _silu(x):
    return x * _sigmoid(x)


def _rms(x, eps):
    return x * lax.rsqrt(jnp.mean(x * x, axis=-1, keepdims=True) + eps)


def _dot(a, b, dims=None, precision=None):
    if dims is None:
        dims = (((a.ndim - 1,), (0,)), ((), ()))
    return lax.dot_general(a, b, dims, precision=precision, preferred_element_type=F32)


def _cumsum_rows(x):
    n = x.shape[0]
    row = lax.broadcasted_iota(jnp.int32, x.shape, 0)
    k = 1
    while k < n:
        x = x + jnp.where(row >= k, pltpu.roll(x, k, 0), 0.0)
        k *= 2
    return x


def _params(*sem):
    return pltpu.CompilerParams(dimension_semantics=sem, vmem_limit_bytes=VMEM_LIMIT)


def _rows_resident_kernel(a_ref, w_ref, o_ref):
    o_ref[...] = _dot(a_ref[...], w_ref[...].astype(BF16)).astype(o_ref.dtype)


def _rows_resident_matmul(a, w, layer, n_blocks, out_dtype, out_block=None, tm=2048, tn=IN_BLOCK,
                          name="matmul"):
    m, kd = a.shape
    out_block = out_block or (lambda j: j)
    return pl.pallas_call(
        _rows_resident_kernel,
        grid=(m // tm, n_blocks),
        in_specs=[pl.BlockSpec((tm, kd), lambda i, j: (i, 0), pipeline_mode=pl.Buffered(1)),
                  pl.BlockSpec((None, kd, tn), lambda i, j: (layer, 0, j))],
        out_specs=pl.BlockSpec((tm, tn), lambda i, j: (i, out_block(j))),
        out_shape=jax.ShapeDtypeStruct((m, n_blocks * tn), out_dtype),
        compiler_params=_params("parallel", "arbitrary"),
        name=name,
    )(a, w)


def _norm_kernel(x_ref, g_ref, h_ref):
    h_ref[...] = (_rms(x_ref[...], 1e-6) * g_ref[...]).astype(BF16)


def _norm(x, g, rows=256):
    t, d = x.shape
    return pl.pallas_call(
        _norm_kernel,
        grid=(t // rows,),
        in_specs=[pl.BlockSpec((rows, d), lambda i: (i, 0)),
                  pl.BlockSpec((1, d), lambda i: (0, 0))],
        out_specs=pl.BlockSpec((rows, d), lambda i: (i, 0)),
        out_shape=jax.ShapeDtypeStruct((t, d), BF16),
        compiler_params=_params("parallel"),
        name="norm",
    )(x, g.reshape(1, d))


def _resid_norm_kernel(x_ref, y_ref, gpost_ref, gnext_ref, xo_ref, h_ref):
    xn = x_ref[...] + _rms(y_ref[...], 1e-6) * gpost_ref[...]
    xo_ref[...] = xn
    h_ref[...] = (_rms(xn, 1e-6) * gnext_ref[...]).astype(BF16)


def _resid_kernel(x_ref, y_ref, gpost_ref, xo_ref):
    xo_ref[...] = x_ref[...] + _rms(y_ref[...], 1e-6) * gpost_ref[...]


def _resid_norm(x, y, g_post, g_next, rows=256):
    t, d = x.shape
    row_spec = pl.BlockSpec((rows, d), lambda i: (i, 0))
    vec_spec = pl.BlockSpec((1, d), lambda i: (0, 0))
    if g_next is None:
        return pl.pallas_call(
            _resid_kernel, grid=(t // rows,),
            in_specs=[row_spec, row_spec, vec_spec],
            out_specs=row_spec,
            out_shape=jax.ShapeDtypeStruct((t, d), F32),
            input_output_aliases={0: 0},
            compiler_params=_params("parallel"),
            name="resid",
        )(x, y, g_post.reshape(1, d)), None
    return pl.pallas_call(
        _resid_norm_kernel, grid=(t // rows,),
        in_specs=[row_spec, row_spec, vec_spec, vec_spec],
        out_specs=[row_spec, row_spec],
        out_shape=[jax.ShapeDtypeStruct((t, d), F32), jax.ShapeDtypeStruct((t, d), BF16)],
        input_output_aliases={0: 0},
        compiler_params=_params("parallel"),
        name="resid_norm",
    )(x, y, g_post.reshape(1, d), g_next.reshape(1, d))


def _merge_kernel(oa_ref, ob_ref, oc_ref, od_ref, gc_ref, wb_ref,
                  wg0_ref, wg1_ref, wg2_ref, wg3_ref, bg0_ref, bg1_ref, bg2_ref, bg3_ref,
                  out_ref):
    gc = gc_ref[...].astype(BF16)
    branches = ((oa_ref, wg0_ref, bg0_ref), (ob_ref, wg1_ref, bg1_ref),
                (oc_ref, wg2_ref, bg2_ref), (od_ref, wg3_ref, bg3_ref))
    acc = None
    for br, (o_ref, wg_ref, bg_ref) in enumerate(branches):
        y = _dot(o_ref[...], wb_ref[br])
        gate = _sigmoid(_dot(gc, wg_ref[...]) + bg_ref[...])
        acc = gate * y if acc is None else acc + gate * y
    out_ref[...] = acc.astype(BF16)


def _merge(outs, tail, w_branch, w_gate_up, b_gate, tm=1024, tn=512):
    t = tail.shape[0]
    d = D_MODEL
    nj = d // tn
    o_spec = pl.BlockSpec((tm, BRANCH_WIDTH), lambda i, j: (i, 0))
    gc_spec = pl.BlockSpec((tm, GATE_RANK), lambda i, j: (i, TAIL_COL["gc"] // GATE_RANK))
    wb_spec = pl.BlockSpec((4, BRANCH_WIDTH, tn), lambda i, j: (0, 0, j))
    wg_specs = [pl.BlockSpec((GATE_RANK, tn), lambda i, j, br=br: (0, br * nj + j)) for br in range(4)]
    bg_specs = [pl.BlockSpec((1, tn), lambda i, j, br=br: (0, br * nj + j)) for br in range(4)]
    bg = b_gate.reshape(1, 4 * d)
    return pl.pallas_call(
        _merge_kernel,
        grid=(t // tm, nj),
        in_specs=[o_spec] * 4 + [gc_spec, wb_spec] + wg_specs + bg_specs,
        out_specs=pl.BlockSpec((tm, tn), lambda i, j: (i, j)),
        out_shape=jax.ShapeDtypeStruct((t, d), BF16),
        compiler_params=_params("parallel", "parallel"),
        name="merge",
    )(*outs, tail, w_branch, *([w_gate_up] * 4), *([bg] * 4))


def _gate_up_kernel(h_ref, wg_ref, wu_ref, o_ref, *, n_real):
    j = pl.program_id(1)

    @pl.when(j < n_real)
    def _():
        h = h_ref[...]
        gate = _dot(h, wg_ref[...].astype(BF16))
        up = _dot(h, wu_ref[...].astype(BF16))
        o_ref[...] = (_silu(gate) * up).astype(BF16)

    @pl.when(j >= n_real)
    def _():
        o_ref[...] = jnp.zeros_like(o_ref)


def _gate_up(h, w_gate, w_up, layer, tm=2048, tn=FFN_BLOCK):
    t, d = h.shape
    n_real = FFN_HIDDEN // tn
    w_spec = pl.BlockSpec((None, d, tn), lambda i, j: (layer, 0, jnp.minimum(j, n_real - 1)))
    return pl.pallas_call(
        functools.partial(_gate_up_kernel, n_real=n_real),
        grid=(t // tm, FFN_PAD // tn),
        in_specs=[pl.BlockSpec((tm, d), lambda i, j: (i, 0), pipeline_mode=pl.Buffered(1)),
                  w_spec, w_spec],
        out_specs=pl.BlockSpec((tm, tn), lambda i, j: (i, j)),
        out_shape=jax.ShapeDtypeStruct((t, FFN_PAD), BF16),
        compiler_params=_params("parallel", "arbitrary"),
        name="ffn_gate_up",
    )(h, w_gate, w_up)


def _down_kernel(u_ref, w_ref, wt_ref, o_ref, *, n_main):
    k = pl.program_id(2)
    u = u_ref[...]

    @pl.when(k == 0)
    def _():
        o_ref[...] = _dot(u, w_ref[...].astype(BF16))

    @pl.when((k > 0) & (k < n_main))
    def _():
        o_ref[...] += _dot(u, w_ref[...].astype(BF16))

    @pl.when(k == n_main)
    def _():
        o_ref[...] += _dot(u, wt_ref[...].astype(BF16))


def _down(u, w_down, w_down_tail, layer, tm=2048, tn=1024, tk=FFN_KBLOCK):
    t = u.shape[0]
    d = w_down.shape[2]
    n_main = FFN_HIDDEN // tk
    return pl.pallas_call(
        functools.partial(_down_kernel, n_main=n_main),
        grid=(t // tm, d // tn, FFN_PAD // tk),
        in_specs=[pl.BlockSpec((tm, tk), lambda i, j, k: (i, k)),
                  pl.BlockSpec((None, tk, tn), lambda i, j, k: (layer, jnp.minimum(k, n_main - 1), j)),
                  pl.BlockSpec((None, tk, tn), lambda i, j, k: (layer, 0, j))],
        out_specs=pl.BlockSpec((tm, tn), lambda i, j, k: (i, j)),
        out_shape=jax.ShapeDtypeStruct((t, d), F32),
        compiler_params=_params("parallel", "parallel", "arbitrary"),
        name="ffn_down",
    )(u, w_down, w_down_tail)


def _lower_bound_kernel(l_ref, o_ref):
    lg = l_ref[...]
    e = jnp.exp(lg - jnp.max(lg, axis=0, keepdims=True))
    p = e / jnp.sum(e, axis=0, keepdims=True)
    run = p[0:1]
    o_ref[0:1, :] = run - p[0:1]
    for l in range(1, DEPTH):
        run = run + p[l:l + 1]
        o_ref[l:l + 1, :] = run - p[0:1]


def _lower_bounds(logits):
    return pl.pallas_call(
        _lower_bound_kernel,
        out_shape=jax.ShapeDtypeStruct(logits.shape, F32),
        name="hgrn_lower_bounds",
    )(logits)


HGRN_LEVELS = (32, 16, 8, 4, 2, 1)


def _hgrn_exponent_matrix():
    c = HGRN_CHUNK
    mat = np.zeros(((len(HGRN_LEVELS) + 2) * c, c), np.float32)
    for li, m in enumerate(HGRN_LEVELS):
        for t in range(c):
            r = (t // (2 * m)) * 2 * m + m - 1
            if (t // m) % 2 == 1:
                mat[li * c + t, r + 1:t + 1] = 1.0
            else:
                mat[li * c + t, t + 1:r + 1] = 1.0
    for t in range(c):
        mat[len(HGRN_LEVELS) * c + t, :t + 1] = 1.0
        mat[(len(HGRN_LEVELS) + 1) * c + t, t + 1:] = 1.0
    return mat


def _hgrn_kernel(q_ref, f_ref, i_ref, g_ref, lb_ref, ng_ref, mat_ref, o_ref, state_ref):
    c = HGRN_CHUNK
    nl = len(HGRN_LEVELS)

    @pl.when(pl.program_id(0) == 0)
    def _():
        state_ref[...] = jnp.zeros_like(state_ref)

    lb = lb_ref[...]
    forget = lb + (1.0 - lb) * _sigmoid(f_ref[...])
    key = 1.0 - forget
    query = _silu(q_ref[...])
    decay = jnp.exp(_dot(mat_ref[...], jnp.log(forget), precision=HIGHEST))

    row = lax.broadcasted_iota(jnp.int32, (c, c), 0)
    col = lax.broadcasted_iota(jnp.int32, (c, c), 1)
    row1 = lax.broadcasted_iota(jnp.int32, (c, 1), 0)
    shifts = [m.bit_length() - 1 for m in HGRN_LEVELS]
    upper = [((row1 >> s) & 1) == 1 for s in shifts]
    same_block = [(row >> (s + 1)) == (col >> (s + 1)) for s in shifts]

    for h in range(HGRN_HEADS):
        sl = slice(h * HGRN_DIM, (h + 1) * HGRN_DIM)
        qh, kh = query[:, sl], key[:, sl]
        vh = i_ref[:, sl].astype(BF16)
        scores = jnp.where(row == col, _dot(qh.astype(BF16), kh.astype(BF16), NT_DIMS), 0.0)
        for li in range(nl):
            dl = decay[li * c:(li + 1) * c, sl]
            qt = jnp.where(upper[li], qh * dl, 0.0).astype(BF16)
            kt = jnp.where(upper[li], 0.0, kh * dl).astype(BF16)
            scores = scores + jnp.where(same_block[li], _dot(qt, kt, NT_DIMS), 0.0)
        q_in = (qh * decay[nl * c:(nl + 1) * c, sl]).astype(BF16)
        k_out = (kh * decay[(nl + 1) * c:(nl + 2) * c, sl]).astype(BF16)
        state_t = state_ref[h]
        o = _dot(scores.astype(BF16), vh) + _dot(q_in, state_t.astype(BF16), NT_DIMS)
        chunk_decay = decay[(nl + 1) * c - 1:(nl + 1) * c, sl]
        state_ref[h] = state_t * chunk_decay + _dot(vh, k_out, TN_DIMS)
        gh = g_ref[:, sl]
        o_ref[:, sl] = (_rms(o, 1e-6) * ng_ref[:, sl] * _silu(gh)).astype(BF16)


def _hgrn(main, lower_bound, norm_g):
    t = main.shape[0]
    c = HGRN_CHUNK
    w = HGRN_HEADS * HGRN_DIM
    mat = jnp.asarray(_hgrn_exponent_matrix())

    def seg(name):
        return pl.BlockSpec((c, w), lambda i, b=MAIN_COL[name] // w: (i, b))

    vec = pl.BlockSpec((1, w), lambda i: (0, 0))
    return pl.pallas_call(
        _hgrn_kernel,
        grid=(t // c,),
        in_specs=[seg("hq"), seg("hf"), seg("hi"), seg("hg"), vec, vec,
                  pl.BlockSpec(mat.shape, lambda i: (0, 0))],
        out_specs=pl.BlockSpec((c, w), lambda i: (i, 0)),
        out_shape=jax.ShapeDtypeStruct((t, w), BF16),
        scratch_shapes=[pltpu.VMEM((HGRN_HEADS, HGRN_DIM, HGRN_DIM), F32)],
        compiler_params=_params("arbitrary"),
        name="hgrn2",
    )(main, main, main, main, lower_bound.reshape(1, w), norm_g.reshape(1, w), mat)


def _attn_kernel(sink_ref, q_ref, k_ref, v_ref, o_ref, prev_ref):
    c = ATTN_BLOCK
    kvw = ATTN_KV_HEADS * ATTN_DIM
    first = pl.program_id(0) == 0

    @pl.when(first)
    def _():
        prev_ref[...] = jnp.zeros_like(prev_ref)

    qi = lax.broadcasted_iota(jnp.int32, (c, 2 * c), 0)
    kj = lax.broadcasted_iota(jnp.int32, (c, 2 * c), 1)
    dist = qi + c - kj
    valid = (dist >= 0) & (dist < c) & ((kj >= c) | jnp.logical_not(first))
    dist_f = dist.astype(F32)

    k_all = jnp.concatenate([prev_ref[:, :kvw], k_ref[...]], axis=0).astype(BF16)
    v_all = jnp.concatenate([prev_ref[:, kvw:], v_ref[...]], axis=0).astype(BF16)
    group = ATTN_Q_HEADS // ATTN_KV_HEADS
    outs = []
    for h in range(ATTN_Q_HEADS):
        g = h // group
        slope = 2.0 ** (-8.0 * (h + 1) / ATTN_Q_HEADS)
        qh = q_ref[:, h * ATTN_DIM:(h + 1) * ATTN_DIM].astype(BF16)
        kg = k_all[:, g * ATTN_DIM:(g + 1) * ATTN_DIM]
        vg = v_all[:, g * ATTN_DIM:(g + 1) * ATTN_DIM]
        logits = _dot(qh, kg, NT_DIMS) * (ATTN_DIM ** -0.5) - slope * dist_f
        logits = jnp.where(valid, logits, NEG_BIG)
        sink = sink_ref[h]
        mx = jnp.maximum(jnp.max(logits, axis=-1, keepdims=True), sink)
        p = jnp.exp(logits - mx)
        denom = jnp.sum(p, axis=-1, keepdims=True) + jnp.exp(sink - mx)
        outs.append(_dot(p.astype(BF16), vg) / denom)
    o_ref[...] = jnp.concatenate(outs, axis=1).astype(BF16)
    prev_ref[:, :kvw] = k_ref[...]
    prev_ref[:, kvw:] = v_ref[...]


def _attn(main, sinks):
    t = main.shape[0]
    c = ATTN_BLOCK
    qw = ATTN_Q_HEADS * ATTN_DIM
    kvw = ATTN_KV_HEADS * ATTN_DIM
    return pl.pallas_call(
        _attn_kernel,
        grid=(t // c,),
        in_specs=[pl.BlockSpec(memory_space=pltpu.SMEM),
                  pl.BlockSpec((c, qw), lambda i: (i, MAIN_COL["aq"] // qw)),
                  pl.BlockSpec((c, kvw), lambda i: (i, MAIN_COL["ak"] // kvw)),
                  pl.BlockSpec((c, kvw), lambda i: (i, MAIN_COL["av"] // kvw))],
        out_specs=pl.BlockSpec((c, qw), lambda i: (i, 0)),
        out_shape=jax.ShapeDtypeStruct((t, qw), BF16),
        scratch_shapes=[pltpu.VMEM((c, 2 * kvw), F32)],
        compiler_params=_params("arbitrary"),
        name="swa",
    )(sinks, main, main, main)


def _ssd_kernel(xs_ref, sb_ref, sc_ref, z_ref, dt_ref, cw_ref, cb_ref, dtb_ref, alog_ref,
                dskip_ref, ng_ref, expand_ref, o_ref, prev_ref, state_ref):
    c = SSD_CHUNK
    width = SSD_HEADS * SSD_DIM
    gw = SSD_GROUPS * SSD_STATE
    per_group = SSD_HEADS // SSD_GROUPS

    @pl.when(pl.program_id(0) == 0)
    def _():
        prev_ref[...] = jnp.zeros_like(prev_ref)
        state_ref[...] = jnp.zeros_like(state_ref)

    def conv_silu(x_ref, lo, hi):
        x = x_ref[...]
        prev = prev_ref[:, lo:hi]
        row = lax.broadcasted_iota(jnp.int32, x.shape, 0)
        conv = x * cw_ref[SSD_CONV - 1:SSD_CONV, lo:hi] + cb_ref[:, lo:hi]
        for back in range(1, SSD_CONV):
            shifted = jnp.where(row >= back, pltpu.roll(x, back, 0), pltpu.roll(prev, back, 0))
            conv = conv + shifted * cw_ref[SSD_CONV - 1 - back:SSD_CONV - back, lo:hi]
        prev_ref[:, lo:hi] = x
        return _silu(conv)

    xs = conv_silu(xs_ref, 0, width)
    bm = conv_silu(sb_ref, width, width + gw).astype(BF16)
    cm = conv_silu(sc_ref, width + gw, width + 2 * gw).astype(BF16)

    dt_in = dt_ref[...] + dtb_ref[...]
    dt = jnp.maximum(dt_in, 0.0) + jnp.log1p(jnp.exp(-jnp.abs(dt_in)))
    b = _cumsum_rows(dt * (-jnp.exp(alog_ref[...])))
    b_t = b.T
    wide = _dot(jnp.concatenate([dt, b], axis=0), expand_ref[...], precision=HIGHEST)
    dt_w, b_w = wide[:c], wide[c:]
    b_last = b_w[c - 1:c, :]
    v = xs * dt_w
    v_bf = v.astype(BF16)
    v_out = (v * jnp.exp(b_last - b_w)).astype(BF16)
    in_scale = jnp.exp(b_w)

    ti = lax.broadcasted_iota(jnp.int32, (c, c), 0)
    si = lax.broadcasted_iota(jnp.int32, (c, c), 1)
    causal = ti >= si
    lane = lax.broadcasted_iota(jnp.int32, (c, 2 * SSD_DIM), 1)
    low = lane < SSD_DIM

    pieces = []
    for g in range(SSD_GROUPS):
        cg = cm[:, g * SSD_STATE:(g + 1) * SSD_STATE]
        bg = bm[:, g * SSD_STATE:(g + 1) * SSD_STATE]
        cb = _dot(cg, bg, NT_DIMS)
        gs = slice(g * per_group * SSD_DIM, (g + 1) * per_group * SSD_DIM)
        state = state_ref[g]
        inter = _dot(cg, state.astype(BF16)) * in_scale[:, gs]
        state_ref[g] = state * jnp.exp(b_last[:, gs]) + _dot(bg, v_out[:, gs], TN_DIMS)
        for pair in range(per_group // 2):
            p_idx = g * (per_group // 2) + pair
            probs = []
            for h in (2 * p_idx, 2 * p_idx + 1):
                w = jnp.exp(jnp.where(causal, b[:, h:h + 1] - b_t[h:h + 1, :], NEG_BIG))
                probs.append((cb * w).astype(BF16))
            v_pair = v_bf[:, p_idx * 2 * SSD_DIM:(p_idx + 1) * 2 * SSD_DIM]
            v_diag = jnp.concatenate([jnp.where(low, v_pair, jnp.zeros_like(v_pair)),
                                      jnp.where(low, jnp.zeros_like(v_pair), v_pair)], axis=0)
            intra = _dot(jnp.concatenate(probs, axis=1), v_diag)
            pieces.append(intra + inter[:, pair * 2 * SSD_DIM:(pair + 1) * 2 * SSD_DIM])
    y = jnp.concatenate(pieces, axis=1) + dskip_ref[...] * xs
    y = y * _silu(z_ref[...])
    o_ref[...] = (_rms(y, 1e-6) * ng_ref[...]).astype(BF16)


def _ssd(main, tail, conv_w, conv_b, dt_bias, a_log, d_skip, norm_g):
    t = main.shape[0]
    c = SSD_CHUNK
    width = SSD_HEADS * SSD_DIM
    gw = SSD_GROUPS * SSD_STATE
    xw = width + 2 * gw

    def pad_lanes(a):
        return jnp.pad(a.reshape(1, -1), ((0, 0), (0, LANES - a.shape[-1])))

    expand = np.zeros((LANES, width), np.float32)
    for h in range(SSD_HEADS):
        expand[h, h * SSD_DIM:(h + 1) * SSD_DIM] = 1.0
    full = lambda shape: pl.BlockSpec(shape, lambda i: (0, 0))
    return pl.pallas_call(
        _ssd_kernel,
        grid=(t // c,),
        in_specs=[pl.BlockSpec((c, width), lambda i: (i, MAIN_COL["xs"] // width)),
                  pl.BlockSpec((c, gw), lambda i: (i, MAIN_COL["sb"] // gw)),
                  pl.BlockSpec((c, gw), lambda i: (i, MAIN_COL["sc"] // gw)),
                  pl.BlockSpec((c, width), lambda i: (i, MAIN_COL["sz"] // width)),
                  pl.BlockSpec((c, LANES), lambda i: (i, TAIL_COL["dt"] // LANES)),
                  full((SSD_CONV, xw)), full((1, xw)), full((1, LANES)), full((1, LANES)),
                  full((1, width)), full((1, width)), full((LANES, width))],
        out_specs=pl.BlockSpec((c, width), lambda i: (i, 0)),
        out_shape=jax.ShapeDtypeStruct((t, width), BF16),
        scratch_shapes=[pltpu.VMEM((c, xw), F32),
                        pltpu.VMEM((SSD_GROUPS, SSD_STATE, width // SSD_GROUPS), F32)],
        compiler_params=_params("arbitrary"),
        name="ssd",
    )(main, main, main, main, tail, conv_w, conv_b.reshape(1, xw), pad_lanes(dt_bias),
      pad_lanes(a_log), jnp.repeat(d_skip, SSD_DIM).reshape(1, width), norm_g.reshape(1, width),
      jnp.asarray(expand))


def _ret_log_gamma(h):
    return math.log(1.0 - 2.0 ** (-5.0 - h))


def _ret_kernel(q_ref, k_ref, v_ref, g_ref, o_ref, state_ref, mask_ref):
    c = RET_CHUNK

    @pl.when(pl.program_id(0) == 0)
    def _():
        state_ref[...] = jnp.zeros_like(state_ref)
        ti = lax.broadcasted_iota(jnp.int32, (c, c), 0)
        si = lax.broadcasted_iota(jnp.int32, (c, c), 1)
        lag = (ti - si).astype(F32)
        for h in range(RET_HEADS):
            mask_ref[h] = jnp.where(ti >= si, jnp.exp(lag * _ret_log_gamma(h)), 0.0)

    pos = lax.broadcasted_iota(jnp.int32, (c, 1), 0).astype(F32)
    for h in range(RET_HEADS):
        lg = _ret_log_gamma(h)
        qh = q_ref[:, h * RET_QK:(h + 1) * RET_QK]
        kh = k_ref[:, h * RET_QK:(h + 1) * RET_QK] * (RET_QK ** -0.5)
        vh = v_ref[:, h * RET_V:(h + 1) * RET_V].astype(BF16)
        scores = _dot(qh.astype(BF16), kh.astype(BF16), NT_DIMS) * mask_ref[h]
        q_in = (qh * jnp.exp((pos + 1.0) * lg)).astype(BF16)
        k_out = (kh * jnp.exp((c - 1.0 - pos) * lg)).astype(BF16)
        state_t = state_ref[h]
        o = _dot(scores.astype(BF16), vh) + _dot(q_in, state_t.astype(BF16), NT_DIMS)
        state_ref[h] = state_t * math.exp(c * lg) + _dot(vh, k_out, TN_DIMS)
        centered = o - jnp.mean(o, axis=-1, keepdims=True)
        normed = centered * lax.rsqrt(jnp.mean(centered * centered, axis=-1, keepdims=True) + 1e-5)
        gh = g_ref[:, h * RET_V:(h + 1) * RET_V]
        o_ref[:, h * RET_V:(h + 1) * RET_V] = (normed * _silu(gh)).astype(BF16)


def _retention(tail):
    t = tail.shape[0]
    c = RET_CHUNK
    qkw = RET_HEADS * RET_QK
    vw = RET_HEADS * RET_V
    return pl.pallas_call(
        _ret_kernel,
        grid=(t // c,),
        in_specs=[pl.BlockSpec((c, qkw), lambda i: (i, TAIL_COL["rq"] // qkw)),
                  pl.BlockSpec((c, qkw), lambda i: (i, TAIL_COL["rk"] // qkw)),
                  pl.BlockSpec((c, vw), lambda i: (i, TAIL_COL["rv"] // vw)),
                  pl.BlockSpec((c, vw), lambda i: (i, TAIL_COL["rg"] // vw))],
        out_specs=pl.BlockSpec((c, vw), lambda i: (i, 0)),
        out_shape=jax.ShapeDtypeStruct((t, vw), BF16),
        scratch_shapes=[pltpu.VMEM((RET_HEADS, RET_V, RET_QK), F32),
                        pltpu.VMEM((RET_HEADS, c, c), F32)],
        compiler_params=_params("arbitrary"),
        name="retention",
    )(tail, tail, tail, tail)


def _w_in_tail(w_in):
    pad = TAIL_WIDTH - (IN_WIDTH - TAIL_SRC) - (TAIL_SRC - DT_SRC)
    return jnp.concatenate([w_in[:, :, TAIL_SRC:], w_in[:, :, DT_SRC:TAIL_SRC],
                            jnp.zeros(w_in.shape[:2] + (pad,), w_in.dtype)], axis=2)


def _w_down_tail(w_down):
    start = (FFN_HIDDEN // FFN_KBLOCK) * FFN_KBLOCK
    return jnp.pad(w_down[:, start:, :], ((0, 0), (0, FFN_PAD - FFN_HIDDEN), (0, 0)))


def kernel(x, norm_mix_pre, norm_mix_post, norm_ffn_pre, norm_ffn_post, w_in,
           hgrn_lb_logits, hgrn_norm, attn_sinks, ssd_conv_w, ssd_conv_b, ssd_dt_bias,
           ssd_a_log, ssd_d, ssd_norm, w_gate_up, b_gate, w_branch, w_out,
           w_ffn_gate, w_ffn_up, w_ffn_down):
    b_, t_, d_ = x.shape
    xr = x.reshape(b_ * t_, d_)
    lower_bounds = _lower_bounds(hgrn_lb_logits)
    w_in_tail = _w_in_tail(w_in)
    w_down_tail = _w_down_tail(w_ffn_down)
    h = _norm(xr, norm_mix_pre[0])
    for l in range(DEPTH):
        main = _rows_resident_matmul(h, w_in, l, MAIN_WIDTH // IN_BLOCK, F32,
                                     out_block=_main_block_perm, name="in_proj_main")
        tail = _rows_resident_matmul(h, w_in_tail, l, TAIL_WIDTH // IN_BLOCK, F32,
                                     name="in_proj_tail")
        outs = (
            _hgrn(main, lower_bounds[l], hgrn_norm[l]),
            _attn(main, attn_sinks[l]),
            _ssd(main, tail, ssd_conv_w[l], ssd_conv_b[l], ssd_dt_bias[l], ssd_a_log[l],
                 ssd_d[l], ssd_norm[l]),
            _retention(tail),
        )
        merged = _merge(outs, tail, w_branch[l].astype(BF16), w_gate_up[l].astype(BF16), b_gate[l])
        y = _rows_resident_matmul(merged, w_out, l, D_MODEL // IN_BLOCK, F32, name="out_proj")
        xr, h = _resid_norm(xr, y, norm_mix_post[l], norm_ffn_pre[l])
        u = _gate_up(h, w_ffn_gate, w_ffn_up, l)
        ff = _down(u, w_ffn_down, w_down_tail, l)
        g_next = norm_mix_pre[l + 1] if l + 1 < DEPTH else None
        xr, h = _resid_norm(xr, ff, norm_ffn_post[l], g_next)
    return xr.reshape(b_, t_, d_)
```

```python
import functools
import math

import numpy as np
import jax
import jax.numpy as jnp
from jax import lax
from jax.experimental import pallas as pl
from jax.experimental.pallas import tpu as pltpu

F32 = jnp.float32
BF16 = jnp.bfloat16
HIGHEST = lax.Precision.HIGHEST

D_MODEL = 4096
DEPTH = 4
BRANCH_WIDTH = D_MODEL // 4
HGRN_HEADS, HGRN_DIM, HGRN_CHUNK = 8, 128, 128
HGRN_HALF = HGRN_CHUNK // 2
ATTN_Q_HEADS, ATTN_KV_HEADS, ATTN_DIM, ATTN_BLOCK = 16, 4, 64, 128
SSD_HEADS, SSD_DIM, SSD_GROUPS, SSD_STATE, SSD_CONV, SSD_CHUNK = 16, 64, 4, 128, 4, 128
RET_HEADS, RET_QK, RET_V, RET_CHUNK = 8, 64, 128, 128
GATE_RANK = 256
FFN_HIDDEN = -(-8 * D_MODEL // (3 * 256)) * 256
FFN_BLOCK = 256
FFN_KBLOCK = 1024
FFN_PAD = -(-FFN_HIDDEN // FFN_KBLOCK) * FFN_KBLOCK
IN_WIDTH = 12048

LANES = 128
VMEM_LIMIT = 56 * 1024 * 1024

IN_BLOCK = 256
_FIRST_ORDER = list(range(20)) + list(range(22, 30)) + [20, 21] + list(range(30, 34))
PROJ_SRC = ([9744 + IN_BLOCK * i for i in range(4)] + [10768 + IN_BLOCK * i for i in range(4)]
            + [IN_BLOCK * j for j in _FIRST_ORDER]
            + [8720, 8720 + IN_BLOCK, 9232, 9232 + IN_BLOCK, 11792, 8704])
PROJ_WIDTH = IN_BLOCK * len(PROJ_SRC)
COL = {"rv": 0, "rg": 1024, "hq": 2048, "hf": 3072, "hi": 4096, "hg": 5120, "aq": 6144,
       "sz": 7168, "xs": 8192, "ak": 9216, "av": 9472, "sb": 9728, "sc": 10240,
       "rq": 10752, "rk": 11264, "gc": 11776, "dt": 12032}


NT_DIMS = (((1,), (1,)), ((), ()))
TN_DIMS = (((0,), (0,)), ((), ()))
NEG_BIG = -1e30


def _sigmoid(x):
    return 1.0 / (1.0 + jnp.exp(-x))


def _silu(x):
    return x * _sigmoid(x)


def _rms(x, eps):
    return x * lax.rsqrt(jnp.mean(x * x, axis=-1, keepdims=True) + eps)


def _dot(a, b, dims=None, precision=None):
    if dims is None:
        dims = (((a.ndim - 1,), (0,)), ((), ()))
    return lax.dot_general(a, b, dims, precision=precision, preferred_element_type=F32)


def _split3(x, axis):
    hi = x.astype(BF16)
    rest = x - hi.astype(F32)
    mid = rest.astype(BF16)
    lo = (rest - mid.astype(F32)).astype(BF16)
    return jnp.concatenate([hi, mid, lo], axis=axis)


def _cumsum_rows(x):
    n = x.shape[0]
    row = lax.broadcasted_iota(jnp.int32, x.shape, 0)
    k = 1
    while k < n:
        x = x + jnp.where(row >= k, pltpu.roll(x, k, 0), 0.0)
        k *= 2
    return x


def _params(*sem):
    return pltpu.CompilerParams(dimension_semantics=sem, vmem_limit_bytes=VMEM_LIMIT)


def _rows_resident_kernel(a_ref, w_ref, o_ref):
    o_ref[...] = _dot(a_ref[...], w_ref[...].astype(BF16)).astype(o_ref.dtype)


def _rows_resident_matmul(a, w, layer, n_blocks, out_dtype, out_block=None, tm=2048, tn=IN_BLOCK,
                          name="matmul"):
    m, kd = a.shape
    out_block = out_block or (lambda j: j)
    return pl.pallas_call(
        _rows_resident_kernel,
        grid=(m // tm, n_blocks),
        in_specs=[pl.BlockSpec((tm, kd), lambda i, j: (i, 0), pipeline_mode=pl.Buffered(1)),
                  pl.BlockSpec((None, kd, tn), lambda i, j: (layer, 0, j))],
        out_specs=pl.BlockSpec((tm, tn), lambda i, j: (i, out_block(j))),
        out_shape=jax.ShapeDtypeStruct((m, n_blocks * tn), out_dtype),
        compiler_params=_params("parallel", "arbitrary"),
        name=name,
    )(a, w)


def _in_proj_kernel(src_ref, a_ref, wt_ref, o_ref):
    del src_ref
    o_ref[...] = _dot(a_ref[...], wt_ref[...].astype(BF16), NT_DIMS)


def _in_proj(h, w_in, layer, tm=2048):
    t, d = h.shape
    w_t = jnp.swapaxes(w_in, 1, 2)
    src = jnp.asarray(PROJ_SRC, jnp.int32)
    grid_spec = pltpu.PrefetchScalarGridSpec(
        num_scalar_prefetch=1,
        grid=(t // tm, len(PROJ_SRC)),
        in_specs=[pl.BlockSpec((tm, d), lambda i, p, src: (i, 0), pipeline_mode=pl.Buffered(1)),
                  pl.BlockSpec((None, pl.Element(IN_BLOCK), pl.Element(d)),
                               lambda i, p, src: (layer, pl.multiple_of(src[p], 8), 0))],
        out_specs=pl.BlockSpec((tm, IN_BLOCK), lambda i, p, src: (i, p)),
    )
    return pl.pallas_call(
        _in_proj_kernel,
        grid_spec=grid_spec,
        out_shape=jax.ShapeDtypeStruct((t, PROJ_WIDTH), F32),
        compiler_params=_params("parallel", "arbitrary"),
        name="in_proj",
    )(src, h, w_t)


def _norm_kernel(x_ref, g_ref, h_ref):
    h_ref[...] = (_rms(x_ref[...], 1e-6) * g_ref[...]).astype(BF16)


def _norm(x, g, rows=256):
    t, d = x.shape
    return pl.pallas_call(
        _norm_kernel,
        grid=(t // rows,),
        in_specs=[pl.BlockSpec((rows, d), lambda i: (i, 0)),
                  pl.BlockSpec((1, d), lambda i: (0, 0))],
        out_specs=pl.BlockSpec((rows, d), lambda i: (i, 0)),
        out_shape=jax.ShapeDtypeStruct((t, d), BF16),
        compiler_params=_params("parallel"),
        name="norm",
    )(x, g.reshape(1, d))


def _resid_norm_kernel(x_ref, y_ref, gpost_ref, gnext_ref, xo_ref, h_ref):
    xn = x_ref[...] + _rms(y_ref[...], 1e-6) * gpost_ref[...]
    xo_ref[...] = xn
    h_ref[...] = (_rms(xn, 1e-6) * gnext_ref[...]).astype(BF16)


def _resid_kernel(x_ref, y_ref, gpost_ref, xo_ref):
    xo_ref[...] = x_ref[...] + _rms(y_ref[...], 1e-6) * gpost_ref[...]


def _resid_norm(x, y, g_post, g_next, rows=256):
    t, d = x.shape
    row_spec = pl.BlockSpec((rows, d), lambda i: (i, 0))
    vec_spec = pl.BlockSpec((1, d), lambda i: (0, 0))
    if g_next is None:
        return pl.pallas_call(
            _resid_kernel, grid=(t // rows,),
            in_specs=[row_spec, row_spec, vec_spec],
            out_specs=row_spec,
            out_shape=jax.ShapeDtypeStruct((t, d), F32),
            compiler_params=_params("parallel"),
            name="resid",
        )(x, y, g_post.reshape(1, d)), None
    return pl.pallas_call(
        _resid_norm_kernel, grid=(t // rows,),
        in_specs=[row_spec, row_spec, vec_spec, vec_spec],
        out_specs=[row_spec, row_spec],
        out_shape=[jax.ShapeDtypeStruct((t, d), F32), jax.ShapeDtypeStruct((t, d), BF16)],
        compiler_params=_params("parallel"),
        name="resid_norm",
    )(x, y, g_post.reshape(1, d), g_next.reshape(1, d))


def _merge_kernel(oa_ref, ob_ref, oc_ref, od_ref, gc_ref, wb_ref,
                  wg0_ref, wg1_ref, wg2_ref, wg3_ref, bg0_ref, bg1_ref, bg2_ref, bg3_ref,
                  out_ref):
    gc = gc_ref[...].astype(BF16)
    branches = ((oa_ref, wg0_ref, bg0_ref), (ob_ref, wg1_ref, bg1_ref),
                (oc_ref, wg2_ref, bg2_ref), (od_ref, wg3_ref, bg3_ref))
    acc = None
    for br, (o_ref, wg_ref, bg_ref) in enumerate(branches):
        y = _dot(o_ref[...], wb_ref[br].astype(BF16))
        gate = _sigmoid(_dot(gc, wg_ref[...].astype(BF16)) + bg_ref[...])
        acc = gate * y if acc is None else acc + gate * y
    out_ref[...] = acc.astype(BF16)


def _merge(outs, proj, w_branch, w_gate_up, b_gate, layer, tm=2048, tn=256):
    t = proj.shape[0]
    d = D_MODEL
    nj = d // tn
    once = pl.Buffered(1)
    o_spec = pl.BlockSpec((tm, BRANCH_WIDTH), lambda i, j: (i, 0), pipeline_mode=once)
    gc_spec = pl.BlockSpec((tm, GATE_RANK), lambda i, j: (i, COL["gc"] // GATE_RANK),
                           pipeline_mode=once)
    wb_spec = pl.BlockSpec((None, 4, BRANCH_WIDTH, tn), lambda i, j: (layer, 0, 0, j))
    wg_specs = [pl.BlockSpec((None, GATE_RANK, tn), lambda i, j, br=br: (layer, 0, br * nj + j))
                for br in range(4)]
    bg_specs = [pl.BlockSpec((None, 1, tn), lambda i, j, br=br: (layer, 0, br * nj + j))
                for br in range(4)]
    bg = b_gate.reshape(DEPTH, 1, 4 * d)
    return pl.pallas_call(
        _merge_kernel,
        grid=(t // tm, nj),
        in_specs=[o_spec] * 4 + [gc_spec, wb_spec] + wg_specs + bg_specs,
        out_specs=pl.BlockSpec((tm, tn), lambda i, j: (i, j)),
        out_shape=jax.ShapeDtypeStruct((t, d), BF16),
        compiler_params=_params("parallel", "arbitrary"),
        name="merge",
    )(*outs, proj, w_branch, *([w_gate_up] * 4), *([bg] * 4))


def _gate_up_kernel(h_ref, wg_ref, wu_ref, o_ref, *, n_real):
    j = pl.program_id(1)

    @pl.when(j < n_real)
    def _():
        h = h_ref[...]
        gate = _dot(h, wg_ref[...].astype(BF16))
        up = _dot(h, wu_ref[...].astype(BF16))
        o_ref[...] = (_silu(gate) * up).astype(BF16)

    @pl.when(j >= n_real)
    def _():
        o_ref[...] = jnp.zeros_like(o_ref)


def _gate_up(h, w_gate, w_up, layer, tm=2048, tn=FFN_BLOCK):
    t, d = h.shape
    n_real = FFN_HIDDEN // tn
    w_spec = pl.BlockSpec((None, d, tn), lambda i, j: (layer, 0, jnp.minimum(j, n_real - 1)))
    return pl.pallas_call(
        functools.partial(_gate_up_kernel, n_real=n_real),
        grid=(t // tm, FFN_PAD // tn),
        in_specs=[pl.BlockSpec((tm, d), lambda i, j: (i, 0), pipeline_mode=pl.Buffered(1)),
                  w_spec, w_spec],
        out_specs=pl.BlockSpec((tm, tn), lambda i, j: (i, j)),
        out_shape=jax.ShapeDtypeStruct((t, FFN_PAD), BF16),
        compiler_params=_params("parallel", "arbitrary"),
        name="ffn_gate_up",
    )(h, w_gate, w_up)


def _down_kernel(u_ref, w_ref, wt_ref, o_ref, *, n_main):
    k = pl.program_id(2)
    u = u_ref[...]

    @pl.when(k == 0)
    def _():
        o_ref[...] = _dot(u, w_ref[...].astype(BF16))

    @pl.when((k > 0) & (k < n_main))
    def _():
        o_ref[...] += _dot(u, w_ref[...].astype(BF16))

    @pl.when(k == n_main)
    def _():
        o_ref[...] += _dot(u, wt_ref[...].astype(BF16))


def _down(u, w_down, w_down_tail, layer, tm=2048, tn=1024, tk=FFN_KBLOCK):
    t = u.shape[0]
    d = w_down.shape[2]
    n_main = FFN_HIDDEN // tk
    return pl.pallas_call(
        functools.partial(_down_kernel, n_main=n_main),
        grid=(t // tm, d // tn, FFN_PAD // tk),
        in_specs=[pl.BlockSpec((tm, tk), lambda i, j, k: (i, k)),
                  pl.BlockSpec((None, tk, tn), lambda i, j, k: (layer, jnp.minimum(k, n_main - 1), j)),
                  pl.BlockSpec((None, tk, tn), lambda i, j, k: (layer, 0, j))],
        out_specs=pl.BlockSpec((tm, tn), lambda i, j, k: (i, j)),
        out_shape=jax.ShapeDtypeStruct((t, d), F32),
        compiler_params=_params("parallel", "parallel", "arbitrary"),
        name="ffn_down",
    )(u, w_down, w_down_tail)


def _lower_bound_kernel(l_ref, o_ref):
    lg = l_ref[...]
    e = jnp.exp(lg - jnp.max(lg, axis=0, keepdims=True))
    p = e / jnp.sum(e, axis=0, keepdims=True)
    run = p[0:1]
    o_ref[0:1, :] = run - p[0:1]
    for l in range(1, DEPTH):
        run = run + p[l:l + 1]
        o_ref[l:l + 1, :] = run - p[0:1]


def _lower_bounds(logits):
    return pl.pallas_call(
        _lower_bound_kernel,
        out_shape=jax.ShapeDtypeStruct(logits.shape, F32),
        name="hgrn_lower_bounds",
    )(logits)


HGRN_LEVELS = (32, 16, 8, 4, 2, 1)


def _hgrn_exponent_matrix():
    c = HGRN_HALF
    mat = np.zeros(((len(HGRN_LEVELS) + 2) * c, c), np.float32)
    for li, m in enumerate(HGRN_LEVELS):
        for t in range(c):
            r = (t // (2 * m)) * 2 * m + m - 1
            if (t // m) % 2 == 1:
                mat[li * c + t, r + 1:t + 1] = 1.0
            else:
                mat[li * c + t, t + 1:r + 1] = 1.0
    for t in range(c):
        mat[len(HGRN_LEVELS) * c + t, :t + 1] = 1.0
        mat[(len(HGRN_LEVELS) + 1) * c + t, t + 1:] = 1.0
    return np.concatenate([mat, mat, mat], axis=1)


def _hgrn_kernel(q_ref, f_ref, i_ref, g_ref, lb_ref, ng_ref, mat_ref, o_ref, state_ref):
    c = HGRN_CHUNK
    hc = HGRN_HALF
    nl = len(HGRN_LEVELS)

    @pl.when(pl.program_id(0) == 0)
    def _():
        state_ref[...] = jnp.zeros_like(state_ref)

    lb = lb_ref[...]
    forget = lb + (1.0 - lb) * _sigmoid(f_ref[...])
    key = 1.0 - forget
    query = _silu(q_ref[...])
    log_f = jnp.log(forget)
    sums = [_dot(mat_ref[...], _split3(log_f[s * hc:(s + 1) * hc], 0)) for s in range(2)]

    def both(group):
        return jnp.concatenate([sm[group * hc:(group + 1) * hc] for sm in sums], axis=0)

    prefix = [sm[nl * hc:(nl + 1) * hc] for sm in sums]
    suffix = [sm[(nl + 1) * hc:(nl + 2) * hc] for sm in sums]
    total = [p[hc - 1:hc] for p in prefix]
    level_exp = [jnp.concatenate([suffix[0], prefix[1]], axis=0)] + [both(li) for li in range(nl)]
    b_in = jnp.exp(jnp.concatenate([prefix[0], prefix[1] + total[0]], axis=0))
    b_out = jnp.exp(jnp.concatenate([suffix[0] + total[1], suffix[1]], axis=0))
    chunk_decay = jnp.exp(total[0] + total[1])

    row = lax.broadcasted_iota(jnp.int32, (c, c), 0)
    col = lax.broadcasted_iota(jnp.int32, (c, c), 1)
    row1 = lax.broadcasted_iota(jnp.int32, (c, 1), 0)
    shifts = [hc.bit_length() - 1] + [m.bit_length() - 1 for m in HGRN_LEVELS]
    upper = [((row1 >> s) & 1) == 1 for s in shifts]
    same_block = [(row >> (s + 1)) == (col >> (s + 1)) for s in shifts]

    for h in range(HGRN_HEADS):
        sl = slice(h * HGRN_DIM, (h + 1) * HGRN_DIM)
        qh, kh = query[:, sl], key[:, sl]
        vh = i_ref[:, sl].astype(BF16)
        scores = jnp.where(row == col, _dot(qh.astype(BF16), kh.astype(BF16), NT_DIMS), 0.0)
        for li in range(nl + 1):
            dl = jnp.exp(level_exp[li][:, sl])
            qt = jnp.where(upper[li], qh * dl, 0.0).astype(BF16)
            kt = jnp.where(upper[li], 0.0, kh * dl).astype(BF16)
            scores = scores + jnp.where(same_block[li], _dot(qt, kt, NT_DIMS), 0.0)
        q_in = (qh * b_in[:, sl]).astype(BF16)
        k_out = (kh * b_out[:, sl]).astype(BF16)
        state_t = state_ref[h]
        o = _dot(scores.astype(BF16), vh) + _dot(q_in, state_t.astype(BF16), NT_DIMS)
        state_ref[h] = state_t * chunk_decay[:, sl] + _dot(vh, k_out, TN_DIMS)
        gh = g_ref[:, sl]
        o_ref[:, sl] = (_rms(o, 1e-6) * ng_ref[:, sl] * _silu(gh)).astype(BF16)


def _hgrn(proj, lower_bound, norm_g):
    t = proj.shape[0]
    c = HGRN_CHUNK
    w = HGRN_HEADS * HGRN_DIM
    mat = jnp.asarray(_hgrn_exponent_matrix(), BF16)

    def seg(name):
        return pl.BlockSpec((c, w), lambda i, b=COL[name] // w: (i, b))

    vec = pl.BlockSpec((1, w), lambda i: (0, 0))
    return pl.pallas_call(
        _hgrn_kernel,
        grid=(t // c,),
        in_specs=[seg("hq"), seg("hf"), seg("hi"), seg("hg"), vec, vec,
                  pl.BlockSpec(mat.shape, lambda i: (0, 0))],
        out_specs=pl.BlockSpec((c, w), lambda i: (i, 0)),
        out_shape=jax.ShapeDtypeStruct((t, w), BF16),
        scratch_shapes=[pltpu.VMEM((HGRN_HEADS, HGRN_DIM, HGRN_DIM), F32)],
        compiler_params=_params("arbitrary"),
        name="hgrn2",
    )(proj, proj, proj, proj, lower_bound.reshape(1, w), norm_g.reshape(1, w), mat)


def _attn_kernel(sink_ref, q_ref, k_ref, v_ref, o_ref, prev_ref):
    c = ATTN_BLOCK
    kvw = ATTN_KV_HEADS * ATTN_DIM
    first = pl.program_id(0) == 0

    @pl.when(first)
    def _():
        prev_ref[...] = jnp.zeros_like(prev_ref)

    qi = lax.broadcasted_iota(jnp.int32, (c, 2 * c), 0)
    kj = lax.broadcasted_iota(jnp.int32, (c, 2 * c), 1)
    dist = qi + c - kj
    valid = (dist >= 0) & (dist < c) & ((kj >= c) | jnp.logical_not(first))
    dist_f = dist.astype(F32)

    k_all = jnp.concatenate([prev_ref[:, :kvw], k_ref[...]], axis=0).astype(BF16)
    v_all = jnp.concatenate([prev_ref[:, kvw:], v_ref[...]], axis=0).astype(BF16)
    group = ATTN_Q_HEADS // ATTN_KV_HEADS
    outs = []
    for h in range(ATTN_Q_HEADS):
        g = h // group
        slope = 2.0 ** (-8.0 * (h + 1) / ATTN_Q_HEADS)
        qh = q_ref[:, h * ATTN_DIM:(h + 1) * ATTN_DIM].astype(BF16)
        kg = k_all[:, g * ATTN_DIM:(g + 1) * ATTN_DIM]
        vg = v_all[:, g * ATTN_DIM:(g + 1) * ATTN_DIM]
        logits = _dot(qh, kg, NT_DIMS) * (ATTN_DIM ** -0.5) - slope * dist_f
        logits = jnp.where(valid, logits, NEG_BIG)
        sink = sink_ref[h]
        mx = jnp.maximum(jnp.max(logits, axis=-1, keepdims=True), sink)
        p = jnp.exp(logits - mx)
        denom = jnp.sum(p, axis=-1, keepdims=True) + jnp.exp(sink - mx)
        outs.append(_dot(p.astype(BF16), vg) / denom)
    o_ref[...] = jnp.concatenate(outs, axis=1).astype(BF16)
    prev_ref[:, :kvw] = k_ref[...]
    prev_ref[:, kvw:] = v_ref[...]


def _attn(proj, sinks):
    t = proj.shape[0]
    c = ATTN_BLOCK
    qw = ATTN_Q_HEADS * ATTN_DIM
    kvw = ATTN_KV_HEADS * ATTN_DIM
    return pl.pallas_call(
        _attn_kernel,
        grid=(t // c,),
        in_specs=[pl.BlockSpec(memory_space=pltpu.SMEM),
                  pl.BlockSpec((c, qw), lambda i: (i, COL["aq"] // qw)),
                  pl.BlockSpec((c, kvw), lambda i: (i, COL["ak"] // kvw)),
                  pl.BlockSpec((c, kvw), lambda i: (i, COL["av"] // kvw))],
        out_specs=pl.BlockSpec((c, qw), lambda i: (i, 0)),
        out_shape=jax.ShapeDtypeStruct((t, qw), BF16),
        scratch_shapes=[pltpu.VMEM((c, 2 * kvw), F32)],
        compiler_params=_params("arbitrary"),
        name="swa",
    )(sinks, proj, proj, proj)


def _ssd_kernel(xs_ref, sb_ref, sc_ref, z_ref, dt_ref, cw_ref, cb_ref, dtb_ref, alog_ref,
                dskip_ref, ng_ref, expand_ref, o_ref, prev_ref, state_ref):
    c = SSD_CHUNK
    width = SSD_HEADS * SSD_DIM
    gw = SSD_GROUPS * SSD_STATE
    per_group = SSD_HEADS // SSD_GROUPS

    @pl.when(pl.program_id(0) == 0)
    def _():
        prev_ref[...] = jnp.zeros_like(prev_ref)
        state_ref[...] = jnp.zeros_like(state_ref)

    def conv_silu(x_ref, lo, hi):
        x = x_ref[...]
        prev = prev_ref[:, lo:hi]
        row = lax.broadcasted_iota(jnp.int32, x.shape, 0)
        conv = x * cw_ref[SSD_CONV - 1:SSD_CONV, lo:hi] + cb_ref[:, lo:hi]
        for back in range(1, SSD_CONV):
            shifted = jnp.where(row >= back, pltpu.roll(x, back, 0), pltpu.roll(prev, back, 0))
            conv = conv + shifted * cw_ref[SSD_CONV - 1 - back:SSD_CONV - back, lo:hi]
        prev_ref[:, lo:hi] = x
        return _silu(conv)

    xs = conv_silu(xs_ref, 0, width)
    bm = conv_silu(sb_ref, width, width + gw).astype(BF16)
    cm = conv_silu(sc_ref, width + gw, width + 2 * gw).astype(BF16)

    dt_in = dt_ref[...] + dtb_ref[...]
    dt = jnp.maximum(dt_in, 0.0) + jnp.log1p(jnp.exp(-jnp.abs(dt_in)))
    b = _cumsum_rows(dt * (-jnp.exp(alog_ref[...])))
    b_t = b.T
    wide = _dot(_split3(jnp.concatenate([dt, b], axis=0), 1), expand_ref[...])
    dt_w, b_w = wide[:c], wide[c:]
    b_last = b_w[c - 1:c, :]
    v = xs * dt_w
    v_bf = v.astype(BF16)
    v_out = (v * jnp.exp(b_last - b_w)).astype(BF16)
    in_scale = jnp.exp(b_w)

    ti = lax.broadcasted_iota(jnp.int32, (c, c), 0)
    si = lax.broadcasted_iota(jnp.int32, (c, c), 1)
    causal = ti >= si
    lane = lax.broadcasted_iota(jnp.int32, (c, 2 * SSD_DIM), 1)
    low = lane < SSD_DIM

    pieces = []
    for g in range(SSD_GROUPS):
        cg = cm[:, g * SSD_STATE:(g + 1) * SSD_STATE]
        bg = bm[:, g * SSD_STATE:(g + 1) * SSD_STATE]
        cb = _dot(cg, bg, NT_DIMS)
        gs = slice(g * per_group * SSD_DIM, (g + 1) * per_group * SSD_DIM)
        state = state_ref[g]
        inter = _dot(cg, state.astype(BF16)) * in_scale[:, gs]
        state_ref[g] = state * jnp.exp(b_last[:, gs]) + _dot(bg, v_out[:, gs], TN_DIMS)
        for pair in range(per_group // 2):
            p_idx = g * (per_group // 2) + pair
            probs = []
            for h in (2 * p_idx, 2 * p_idx + 1):
                w = jnp.exp(jnp.where(causal, b[:, h:h + 1] - b_t[h:h + 1, :], NEG_BIG))
                probs.append((cb * w).astype(BF16))
            v_pair = v_bf[:, p_idx * 2 * SSD_DIM:(p_idx + 1) * 2 * SSD_DIM]
            v_diag = jnp.concatenate([jnp.where(low, v_pair, jnp.zeros_like(v_pair)),
                                      jnp.where(low, jnp.zeros_like(v_pair), v_pair)], axis=0)
            intra = _dot(jnp.concatenate(probs, axis=1), v_diag)
            pieces.append(intra + inter[:, pair * 2 * SSD_DIM:(pair + 1) * 2 * SSD_DIM])
    y = jnp.concatenate(pieces, axis=1) + dskip_ref[...] * xs
    y = y * _silu(z_ref[...])
    o_ref[...] = (_rms(y, 1e-6) * ng_ref[...]).astype(BF16)


def _ssd(proj, conv_w, conv_b, dt_bias, a_log, d_skip, norm_g):
    t = proj.shape[0]
    c = SSD_CHUNK
    width = SSD_HEADS * SSD_DIM
    gw = SSD_GROUPS * SSD_STATE
    xw = width + 2 * gw

    def pad_lanes(a):
        return jnp.pad(a.reshape(1, -1), ((0, 0), (0, LANES - a.shape[-1])))

    expand = np.zeros((LANES, width), np.float32)
    for h in range(SSD_HEADS):
        expand[h, h * SSD_DIM:(h + 1) * SSD_DIM] = 1.0
    expand = jnp.asarray(np.concatenate([expand] * 3, axis=0), BF16)
    full =lambda shape: pl.BlockSpec(shape, lambda i: (0, 0))
    return pl.pallas_call(
        _ssd_kernel,
        grid=(t // c,),
        in_specs=[pl.BlockSpec((c, width), lambda i: (i, COL["xs"] // width)),
                  pl.BlockSpec((c, gw), lambda i: (i, COL["sb"] // gw)),
                  pl.BlockSpec((c, gw), lambda i: (i, COL["sc"] // gw)),
                  pl.BlockSpec((c, width), lambda i: (i, COL["sz"] // width)),
                  pl.BlockSpec((c, LANES), lambda i: (i, COL["dt"] // LANES)),
                  full((SSD_CONV, xw)), full((1, xw)), full((1, LANES)), full((1, LANES)),
                  full((1, width)), full((1, width)), full((3 * LANES, width))],
        out_specs=pl.BlockSpec((c, width), lambda i: (i, 0)),
        out_shape=jax.ShapeDtypeStruct((t, width), BF16),
        scratch_shapes=[pltpu.VMEM((c, xw), F32),
                        pltpu.VMEM((SSD_GROUPS, SSD_STATE, width // SSD_GROUPS), F32)],
        compiler_params=_params("arbitrary"),
        name="ssd",
    )(proj, proj, proj, proj, proj, conv_w, conv_b.reshape(1, xw), pad_lanes(dt_bias),
      pad_lanes(a_log), jnp.repeat(d_skip, SSD_DIM).reshape(1, width), norm_g.reshape(1, width),
      expand)


def _ret_log_gamma(h):
    return math.log(1.0 - 2.0 ** (-5.0 - h))


def _ret_kernel(q_ref, k_ref, v_ref, g_ref, o_ref, state_ref, mask_ref):
    c = RET_CHUNK

    @pl.when(pl.program_id(0) == 0)
    def _():
        state_ref[...] = jnp.zeros_like(state_ref)
        ti = lax.broadcasted_iota(jnp.int32, (c, c), 0)
        si = lax.broadcasted_iota(jnp.int32, (c, c), 1)
        lag = (ti - si).astype(F32)
        for h in range(RET_HEADS):
            mask_ref[h] = jnp.where(ti >= si, jnp.exp(lag * _ret_log_gamma(h)), 0.0)

    pos = lax.broadcasted_iota(jnp.int32, (c, 1), 0).astype(F32)
    for h in range(RET_HEADS):
        lg = _ret_log_gamma(h)
        qh = q_ref[:, h * RET_QK:(h + 1) * RET_QK]
        kh = k_ref[:, h * RET_QK:(h + 1) * RET_QK] * (RET_QK ** -0.5)
        vh = v_ref[:, h * RET_V:(h + 1) * RET_V].astype(BF16)
        scores = _dot(qh.astype(BF16), kh.astype(BF16), NT_DIMS) * mask_ref[h]
        q_in = (qh * jnp.exp((pos + 1.0) * lg)).astype(BF16)
        k_out = (kh * jnp.exp((c - 1.0 - pos) * lg)).astype(BF16)
        state_t = state_ref[h]
        o = _dot(scores.astype(BF16), vh) + _dot(q_in, state_t.astype(BF16), NT_DIMS)
        state_ref[h] = state_t * math.exp(c * lg) + _dot(vh, k_out, TN_DIMS)
        centered = o - jnp.mean(o, axis=-1, keepdims=True)
        normed = centered * lax.rsqrt(jnp.mean(centered * centered, axis=-1, keepdims=True) + 1e-5)
        gh = g_ref[:, h * RET_V:(h + 1) * RET_V]
        o_ref[:, h * RET_V:(h + 1) * RET_V] = (normed * _silu(gh)).astype(BF16)


def _retention(proj):
    t = proj.shape[0]
    c = RET_CHUNK
    qkw = RET_HEADS * RET_QK
    vw = RET_HEADS * RET_V
    return pl.pallas_call(
        _ret_kernel,
        grid=(t // c,),
        in_specs=[pl.BlockSpec((c, qkw), lambda i: (i, COL["rq"] // qkw)),
                  pl.BlockSpec((c, qkw), lambda i: (i, COL["rk"] // qkw)),
                  pl.BlockSpec((c, vw), lambda i: (i, COL["rv"] // vw)),
                  pl.BlockSpec((c, vw), lambda i: (i, COL["rg"] // vw))],
        out_specs=pl.BlockSpec((c, vw), lambda i: (i, 0)),
        out_shape=jax.ShapeDtypeStruct((t, vw), BF16),
        scratch_shapes=[pltpu.VMEM((RET_HEADS, RET_V, RET_QK), F32),
                        pltpu.VMEM((RET_HEADS, c, c), F32)],
        compiler_params=_params("arbitrary"),
        name="retention",
    )(proj, proj, proj, proj)


def _w_down_tail(w_down):
    start = (FFN_HIDDEN // FFN_KBLOCK) * FFN_KBLOCK
    return jnp.pad(w_down[:, start:, :], ((0, 0), (0, FFN_PAD - FFN_HIDDEN), (0, 0)))


def kernel(x, norm_mix_pre, norm_mix_post, norm_ffn_pre, norm_ffn_post, w_in,
           hgrn_lb_logits, hgrn_norm, attn_sinks, ssd_conv_w, ssd_conv_b, ssd_dt_bias,
           ssd_a_log, ssd_d, ssd_norm, w_gate_up, b_gate, w_branch, w_out,
           w_ffn_gate, w_ffn_up, w_ffn_down):
    b_, t_, d_ = x.shape
    xr = x.reshape(b_ * t_, d_)
    lower_bounds = _lower_bounds(hgrn_lb_logits)
    w_down_tail = _w_down_tail(w_ffn_down)
    h = _norm(xr, norm_mix_pre[0])
    for l in range(DEPTH):
        proj = _in_proj(h, w_in, l)
        outs = (
            _hgrn(proj, lower_bounds[l], hgrn_norm[l]),
            _attn(proj, attn_sinks[l]),
            _ssd(proj, ssd_conv_w[l], ssd_conv_b[l], ssd_dt_bias[l], ssd_a_log[l], ssd_d[l],
                 ssd_norm[l]),
            _retention(proj),
        )
        merged = _merge(outs, proj, w_branch, w_gate_up, b_gate, l)
        y = _rows_resident_matmul(merged, w_out, l, D_MODEL // IN_BLOCK, F32, name="out_proj")
        xr, h = _resid_norm(xr, y, norm_mix_post[l], norm_ffn_pre[l])
        u = _gate_up(h, w_ffn_gate, w_ffn_up, l)
        ff = _down(u, w_ffn_down, w_down_tail, l)
        g_next = norm_mix_pre[l + 1] if l + 1 < DEPTH else None
        xr, h = _resid_norm(xr, ff, norm_ffn_post[l], g_next)
    return xr.reshape(b_, t_, d_)
```

```python
import functools
import math

import numpy as np
import jax
import jax.numpy as jnp
from jax import lax
from jax.experimental import pallas as pl
from jax.experimental.pallas import tpu as pltpu

F32 = jnp.float32
BF16 = jnp.bfloat16
HIGHEST = lax.Precision.HIGHEST

D_MODEL = 4096
DEPTH = 4
BRANCH_WIDTH = D_MODEL // 4
HGRN_HEADS, HGRN_DIM, HGRN_CHUNK = 8, 128, 128
HGRN_HALF = HGRN_CHUNK // 2
ATTN_Q_HEADS, ATTN_KV_HEADS, ATTN_DIM, ATTN_BLOCK = 16, 4, 64, 128
SSD_HEADS, SSD_DIM, SSD_GROUPS, SSD_STATE, SSD_CONV, SSD_CHUNK = 16, 64, 4, 128, 4, 128
SSD_TAIL = 8
RET_HEADS, RET_QK, RET_V, RET_CHUNK = 8, 64, 128, 128
GATE_RANK = 256
FFN_HIDDEN = -(-8 * D_MODEL // (3 * 256)) * 256
FFN_BLOCK = 256
FFN_KBLOCK = 1024
FFN_PAD = -(-FFN_HIDDEN // FFN_KBLOCK) * FFN_KBLOCK
IN_WIDTH = 12048

LANES = 128
BF16_ROWS = 16
VMEM_LIMIT = 56 * 1024 * 1024

IN_BLOCK = 256
_FIRST_ORDER = list(range(20)) + list(range(22, 30)) + [20, 21] + list(range(30, 34))
PROJ_SRC = ([9744 + IN_BLOCK * i for i in range(4)] + [10768 + IN_BLOCK * i for i in range(4)]
            + [IN_BLOCK * j for j in _FIRST_ORDER]
            + [8720, 8720 + IN_BLOCK, 9232, 9232 + IN_BLOCK, 11792, 8704])
PROJ_WIDTH = IN_BLOCK * len(PROJ_SRC)
COL = {"rv": 0, "rg": 1024, "hq": 2048, "hf": 3072, "hi": 4096, "hg": 5120, "aq": 6144,
       "sz": 7168, "xs": 8192, "ak": 9216, "av": 9472, "sb": 9728, "sc": 10240,
       "rq": 10752, "rk": 11264, "gc": 11776, "dt": 12032}


NT_DIMS = (((1,), (1,)), ((), ()))
TN_DIMS = (((0,), (0,)), ((), ()))
NEG_BIG = -1e30


def _sigmoid(x):
    return 1.0 / (1.0 + jnp.exp(-x))


def _silu(x):
    return x * _sigmoid(x)


def _rms(x, eps):
    return x * lax.rsqrt(jnp.mean(x * x, axis=-1, keepdims=True) + eps)


def _dot(a, b, dims=None, precision=None):
    if dims is None:
        dims = (((a.ndim - 1,), (0,)), ((), ()))
    return lax.dot_general(a, b, dims, precision=precision, preferred_element_type=F32)


def _split3(x, axis):
    hi = x.astype(BF16)
    rest = x - hi.astype(F32)
    mid = rest.astype(BF16)
    lo = (rest - mid.astype(F32)).astype(BF16)
    return jnp.concatenate([hi, mid, lo], axis=axis)


def _cumsum_rows(x):
    n = x.shape[0]
    row = lax.broadcasted_iota(jnp.int32, x.shape, 0)
    k = 1
    while k < n:
        x = x + jnp.where(row >= k, pltpu.roll(x, k, 0), 0.0)
        k *= 2
    return x


def _params(*sem):
    return pltpu.CompilerParams(dimension_semantics=sem, vmem_limit_bytes=VMEM_LIMIT)


def _rows_resident_kernel(a_ref, w_ref, o_ref):
    o_ref[...] = _dot(a_ref[...], w_ref[...].astype(BF16)).astype(o_ref.dtype)


def _rows_resident_matmul(a, w, layer, n_blocks, out_dtype, out_block=None, tm=2048, tn=IN_BLOCK,
                          name="matmul"):
    m, kd = a.shape
    out_block = out_block or (lambda j: j)
    return pl.pallas_call(
        _rows_resident_kernel,
        grid=(m // tm, n_blocks),
        in_specs=[pl.BlockSpec((tm, kd), lambda i, j: (i, 0)),
                  pl.BlockSpec((None, kd, tn), lambda i, j: (layer, 0, j))],
        out_specs=pl.BlockSpec((tm, tn), lambda i, j: (i, out_block(j))),
        out_shape=jax.ShapeDtypeStruct((m, n_blocks * tn), out_dtype),
        compiler_params=_params("parallel", "arbitrary"),
        name=name,
    )(a, w)


def _in_proj_kernel(src_ref, a_ref, wt_ref, o_ref):
    del src_ref
    o_ref[...] = _dot(a_ref[...], wt_ref[...].astype(BF16), NT_DIMS)


def _in_proj(h, w_in, layer, tm=2048):
    t, d = h.shape
    w_t = jnp.swapaxes(w_in, 1, 2)
    src = jnp.asarray(PROJ_SRC, jnp.int32)
    grid_spec = pltpu.PrefetchScalarGridSpec(
        num_scalar_prefetch=1,
        grid=(t // tm, len(PROJ_SRC)),
        in_specs=[pl.BlockSpec((tm, d), lambda i, p, src: (i, 0)),
                  pl.BlockSpec((None, pl.Element(IN_BLOCK), pl.Element(d)),
                               lambda i, p, src: (layer, pl.multiple_of(src[p], 8), 0))],
        out_specs=pl.BlockSpec((tm, IN_BLOCK), lambda i, p, src: (i, p)),
    )
    return pl.pallas_call(
        _in_proj_kernel,
        grid_spec=grid_spec,
        out_shape=jax.ShapeDtypeStruct((t, PROJ_WIDTH), F32),
        compiler_params=_params("parallel", "arbitrary"),
        name="in_proj",
    )(src, h, w_t)


def _norm_kernel(x_ref, g_ref, h_ref):
    h_ref[...] = (_rms(x_ref[...], 1e-6) * g_ref[...]).astype(BF16)


def _norm(x, g, rows=256):
    t, d = x.shape
    return pl.pallas_call(
        _norm_kernel,
        grid=(t // rows,),
        in_specs=[pl.BlockSpec((rows, d), lambda i: (i, 0)),
                  pl.BlockSpec((1, d), lambda i: (0, 0))],
        out_specs=pl.BlockSpec((rows, d), lambda i: (i, 0)),
        out_shape=jax.ShapeDtypeStruct((t, d), BF16),
        compiler_params=_params("parallel"),
        name="norm",
    )(x, g.reshape(1, d))


def _resid_norm_kernel(x_ref, y_ref, gpost_ref, gnext_ref, xo_ref, h_ref):
    xn = x_ref[...] + _rms(y_ref[...], 1e-6) * gpost_ref[...]
    xo_ref[...] = xn
    h_ref[...] = (_rms(xn, 1e-6) * gnext_ref[...]).astype(BF16)


def _resid_kernel(x_ref, y_ref, gpost_ref, xo_ref):
    xo_ref[...] = x_ref[...] + _rms(y_ref[...], 1e-6) * gpost_ref[...]


def _resid_norm(x, y, g_post, g_next, rows=256):
    t, d = x.shape
    row_spec = pl.BlockSpec((rows, d), lambda i: (i, 0))
    vec_spec = pl.BlockSpec((1, d), lambda i: (0, 0))
    if g_next is None:
        return pl.pallas_call(
            _resid_kernel, grid=(t // rows,),
            in_specs=[row_spec, row_spec, vec_spec],
            out_specs=row_spec,
            out_shape=jax.ShapeDtypeStruct((t, d), F32),
            compiler_params=_params("parallel"),
            name="resid",
        )(x, y, g_post.reshape(1, d)), None
    return pl.pallas_call(
        _resid_norm_kernel, grid=(t // rows,),
        in_specs=[row_spec, row_spec, vec_spec, vec_spec],
        out_specs=[row_spec, row_spec],
        out_shape=[jax.ShapeDtypeStruct((t, d), F32), jax.ShapeDtypeStruct((t, d), BF16)],
        compiler_params=_params("parallel"),
        name="resid_norm",
    )(x, y, g_post.reshape(1, d), g_next.reshape(1, d))


def _merge_kernel(oa_ref, ob_ref, oc_ref, od_ref, gc_ref, wb_ref,
                  wg0_ref, wg1_ref, wg2_ref, wg3_ref, bg0_ref, bg1_ref, bg2_ref, bg3_ref,
                  out_ref):
    gc = gc_ref[...].astype(BF16)
    branches = ((oa_ref, wg0_ref, bg0_ref), (ob_ref, wg1_ref, bg1_ref),
                (oc_ref, wg2_ref, bg2_ref), (od_ref, wg3_ref, bg3_ref))
    acc = None
    for br, (o_ref, wg_ref, bg_ref) in enumerate(branches):
        y = _dot(o_ref[...], wb_ref[br].astype(BF16))
        gate = _sigmoid(_dot(gc, wg_ref[...].astype(BF16)) + bg_ref[...])
        acc = gate * y if acc is None else acc + gate * y
    out_ref[...] = acc.astype(BF16)


def _merge(outs, proj, w_branch, w_gate_up, b_gate, layer, tm=2048, tn=256):
    t = proj.shape[0]
    d = D_MODEL
    nj = d // tn
    once = pl.Buffered(1)
    o_spec = pl.BlockSpec((tm, BRANCH_WIDTH), lambda i, j: (i, 0), pipeline_mode=once)
    gc_spec = pl.BlockSpec((tm, GATE_RANK), lambda i, j: (i, COL["gc"] // GATE_RANK),
                           pipeline_mode=once)
    wb_spec = pl.BlockSpec((None, 4, BRANCH_WIDTH, tn), lambda i, j: (layer, 0, 0, j))
    wg_specs = [pl.BlockSpec((None, GATE_RANK, tn), lambda i, j, br=br: (layer, 0, br * nj + j))
                for br in range(4)]
    bg_specs = [pl.BlockSpec((None, 1, tn), lambda i, j, br=br: (layer, 0, br * nj + j))
                for br in range(4)]
    bg = b_gate.reshape(DEPTH, 1, 4 * d)
    return pl.pallas_call(
        _merge_kernel,
        grid=(t // tm, nj),
        in_specs=[o_spec] * 4 + [gc_spec, wb_spec] + wg_specs + bg_specs,
        out_specs=pl.BlockSpec((tm, tn), lambda i, j: (i, j)),
        out_shape=jax.ShapeDtypeStruct((t, d), BF16),
        compiler_params=_params("parallel", "arbitrary"),
        name="merge",
    )(*outs, proj, w_branch, *([w_gate_up] * 4), *([bg] * 4))


def _gate_up_kernel(h_ref, wg_ref, wu_ref, o_ref, *, n_real):
    j = pl.program_id(1)

    @pl.when(j < n_real)
    def _():
        h = h_ref[...]
        gate = _dot(h, wg_ref[...].astype(BF16))
        up = _dot(h, wu_ref[...].astype(BF16))
        o_ref[...] = (_silu(gate) * up).astype(BF16)

    @pl.when(j >= n_real)
    def _():
        o_ref[...] = jnp.zeros_like(o_ref)


def _gate_up(h, w_gate, w_up, layer, tm=2048, tn=FFN_BLOCK):
    t, d = h.shape
    n_real = FFN_HIDDEN // tn
    w_spec = pl.BlockSpec((None, d, tn), lambda i, j: (layer, 0, jnp.minimum(j, n_real - 1)))
    return pl.pallas_call(
        functools.partial(_gate_up_kernel, n_real=n_real),
        grid=(t // tm, FFN_PAD // tn),
        in_specs=[pl.BlockSpec((tm, d), lambda i, j: (i, 0), pipeline_mode=pl.Buffered(1)),
                  w_spec, w_spec],
        out_specs=pl.BlockSpec((tm, tn), lambda i, j: (i, j)),
        out_shape=jax.ShapeDtypeStruct((t, FFN_PAD), BF16),
        compiler_params=_params("parallel", "arbitrary"),
        name="ffn_gate_up",
    )(h, w_gate, w_up)


def _down_kernel(u_ref, w_ref, wt_ref, o_ref, *, n_main):
    k = pl.program_id(2)
    u = u_ref[...]

    @pl.when(k == 0)
    def _():
        o_ref[...] = _dot(u, w_ref[...].astype(BF16))

    @pl.when((k > 0) & (k < n_main))
    def _():
        o_ref[...] += _dot(u, w_ref[...].astype(BF16))

    @pl.when(k == n_main)
    def _():
        o_ref[...] += _dot(u, wt_ref[...].astype(BF16))


def _down(u, w_down, w_down_tail, layer, tm=2048, tn=1024, tk=FFN_KBLOCK):
    t = u.shape[0]
    d = w_down.shape[2]
    n_main = FFN_HIDDEN // tk
    return pl.pallas_call(
        functools.partial(_down_kernel, n_main=n_main),
        grid=(t // tm, d // tn, FFN_PAD // tk),
        in_specs=[pl.BlockSpec((tm, tk), lambda i, j, k: (i, k)),
                  pl.BlockSpec((None, tk, tn), lambda i, j, k: (layer, jnp.minimum(k, n_main - 1), j)),
                  pl.BlockSpec((None, tk, tn), lambda i, j, k: (layer, 0, j))],
        out_specs=pl.BlockSpec((tm, tn), lambda i, j, k: (i, j)),
        out_shape=jax.ShapeDtypeStruct((t, d), F32),
        compiler_params=_params("parallel", "parallel", "arbitrary"),
        name="ffn_down",
    )(u, w_down, w_down_tail)


def _lower_bound_kernel(l_ref, o_ref):
    lg = l_ref[...]
    e = jnp.exp(lg - jnp.max(lg, axis=0, keepdims=True))
    p = e / jnp.sum(e, axis=0, keepdims=True)
    run = p[0:1]
    o_ref[0:1, :] = run - p[0:1]
    for l in range(1, DEPTH):
        run = run + p[l:l + 1]
        o_ref[l:l + 1, :] = run - p[0:1]


def _lower_bounds(logits):
    return pl.pallas_call(
        _lower_bound_kernel,
        out_shape=jax.ShapeDtypeStruct(logits.shape, F32),
        name="hgrn_lower_bounds",
    )(logits)


HGRN_LEVELS = (32, 16, 8, 4, 2, 1)


def _hgrn_exponent_matrix():
    c = HGRN_HALF
    mat = np.zeros(((len(HGRN_LEVELS) + 2) * c, c), np.float32)
    for li, m in enumerate(HGRN_LEVELS):
        for t in range(c):
            r = (t // (2 * m)) * 2 * m + m - 1
            if (t // m) % 2 == 1:
                mat[li * c + t, r + 1:t + 1] = 1.0
            else:
                mat[li * c + t, t + 1:r + 1] = 1.0
    for t in range(c):
        mat[len(HGRN_LEVELS) * c + t, :t + 1] = 1.0
        mat[(len(HGRN_LEVELS) + 1) * c + t, t + 1:] = 1.0
    return np.concatenate([mat, mat, mat], axis=1)


def _hgrn_body(q_ref, f_ref, i_ref, g_ref, lb_ref, ng_ref, mat_ref, o_ref, state_ref):
    c = HGRN_CHUNK
    hc = HGRN_HALF
    nl = len(HGRN_LEVELS)

    lb = lb_ref[...]
    forget = lb + (1.0 - lb) * _sigmoid(f_ref[...])
    key = 1.0 - forget
    query = _silu(q_ref[...])
    log_f = jnp.log(forget)
    sums = [_dot(mat_ref[...], _split3(log_f[s * hc:(s + 1) * hc], 0)) for s in range(2)]

    def both(group):
        return jnp.concatenate([sm[group * hc:(group + 1) * hc] for sm in sums], axis=0)

    prefix = [sm[nl * hc:(nl + 1) * hc] for sm in sums]
    suffix = [sm[(nl + 1) * hc:(nl + 2) * hc] for sm in sums]
    total = [p[hc - 1:hc] for p in prefix]
    level_exp = [jnp.concatenate([suffix[0], prefix[1]], axis=0)] + [both(li) for li in range(nl)]
    b_in = jnp.exp(jnp.concatenate([prefix[0], prefix[1] + total[0]], axis=0))
    b_out = jnp.exp(jnp.concatenate([suffix[0] + total[1], suffix[1]], axis=0))
    chunk_decay = jnp.exp(total[0] + total[1])

    row = lax.broadcasted_iota(jnp.int32, (c, c), 0)
    col = lax.broadcasted_iota(jnp.int32, (c, c), 1)
    row1 = lax.broadcasted_iota(jnp.int32, (c, 1), 0)
    half_sizes = (hc,) + HGRN_LEVELS
    shifts = [m.bit_length() - 1 for m in half_sizes]
    upper = [((row1 >> s) & 1) == 1 for s in shifts]
    same_block = [(row >> (s + 1)) == (col >> (s + 1)) for s in shifts]

    for h in range(HGRN_HEADS):
        sl = slice(h * HGRN_DIM, (h + 1) * HGRN_DIM)
        qh, kh = query[:, sl], key[:, sl]
        vh = i_ref[:, sl].astype(BF16)
        scores = jnp.where(row == col, _dot(qh.astype(BF16), kh.astype(BF16), NT_DIMS), 0.0)
        for li, m in enumerate(half_sizes):
            dl = jnp.exp(level_exp[li][:, sl])
            if m % BF16_ROWS == 0:
                zero = jnp.zeros((m, HGRN_DIM), BF16)
                q_parts, k_parts = [], []
                for blk in range(c // m):
                    rs = slice(blk * m, (blk + 1) * m)
                    is_upper = blk % 2 == 1
                    q_parts.append((qh[rs] * dl[rs]).astype(BF16) if is_upper else zero)
                    k_parts.append(zero if is_upper else (kh[rs] * dl[rs]).astype(BF16))
                qt = jnp.concatenate(q_parts, axis=0)
                kt = jnp.concatenate(k_parts, axis=0)
            else:
                qt = jnp.where(upper[li], qh * dl, 0.0).astype(BF16)
                kt = jnp.where(upper[li], 0.0, kh * dl).astype(BF16)
            level = _dot(qt, kt, NT_DIMS)
            scores = scores + (level if 2 * m == c else jnp.where(same_block[li], level, 0.0))
        q_in = (qh * b_in[:, sl]).astype(BF16)
        k_out = (kh * b_out[:, sl]).astype(BF16)
        state_t = state_ref[h]
        o = _dot(scores.astype(BF16), vh) + _dot(q_in, state_t.astype(BF16), NT_DIMS)
        state_ref[h] = state_t * chunk_decay[:, sl] + _dot(vh, k_out, TN_DIMS)
        gh = g_ref[:, sl]
        o_ref[:, sl] = (_rms(o, 1e-6) * ng_ref[:, sl] * _silu(gh)).astype(BF16)


def _attn_slope(h):
    return 2.0 ** (-8.0 * (h + 1) / ATTN_Q_HEADS)


def _attn_init_bias(bias_ref):
    c = ATTN_BLOCK
    qi = lax.broadcasted_iota(jnp.int32, (c, 2 * c), 0)
    kj = lax.broadcasted_iota(jnp.int32, (c, 2 * c), 1)
    dist = qi + c - kj
    in_window = (dist >= 0) & (dist < c)
    dist_f = dist.astype(F32)
    for h in range(ATTN_Q_HEADS):
        alibi = -_attn_slope(h) * dist_f
        bias_ref[0, h] = jnp.where(in_window & (kj >= c), alibi, NEG_BIG)
        bias_ref[1, h] = jnp.where(in_window, alibi, NEG_BIG)


def _attn_body(sink_ref, q_ref, k_ref, v_ref, o_ref, prev_ref, bias_ref):
    c = ATTN_BLOCK
    kvw = ATTN_KV_HEADS * ATTN_DIM
    which = jnp.where(pl.program_id(0) == 0, 0, 1)

    k_all =jnp.concatenate([prev_ref[:, :kvw], k_ref[...]], axis=0).astype(BF16)
    v_all = jnp.concatenate([prev_ref[:, kvw:], v_ref[...]], axis=0).astype(BF16)
    group = ATTN_Q_HEADS // ATTN_KV_HEADS
    outs = []
    for h in range(ATTN_Q_HEADS):
        g = h // group
        qh = (q_ref[:, h * ATTN_DIM:(h + 1) * ATTN_DIM] * (ATTN_DIM ** -0.5)).astype(BF16)
        kg = k_all[:, g * ATTN_DIM:(g + 1) * ATTN_DIM]
        vg = v_all[:, g * ATTN_DIM:(g + 1) * ATTN_DIM]
        logits = _dot(qh, kg, NT_DIMS) + bias_ref[which, h]
        sink = sink_ref[h]
        mx = jnp.maximum(jnp.max(logits, axis=-1, keepdims=True), sink)
        p = jnp.exp(logits - mx)
        denom = jnp.sum(p, axis=-1, keepdims=True) + jnp.exp(sink - mx)
        outs.append(_dot(p.astype(BF16), vg) / denom)
    o_ref[...] = jnp.concatenate(outs, axis=1).astype(BF16)
    prev_ref[:, :kvw] = k_ref[...]
    prev_ref[:, kvw:] = v_ref[...]


def _ssd_body(xs_ref, sb_ref, sc_ref, z_ref, dt_ref, cw_ref, cb_ref, dtb_ref, alog_ref,
              dskip_ref, ng_ref, expand_ref, o_ref, prev_ref, state_ref):
    c = SSD_CHUNK
    width = SSD_HEADS * SSD_DIM
    gw = SSD_GROUPS * SSD_STATE
    per_group = SSD_HEADS // SSD_GROUPS

    def conv_silu(x_ref, lo, hi):
        x = x_ref[...]
        ext = jnp.concatenate([prev_ref[:, lo:hi], x], axis=0)
        conv = x * cw_ref[SSD_CONV - 1:SSD_CONV, lo:hi] + cb_ref[:, lo:hi]
        for back in range(1, SSD_CONV):
            shifted = ext[SSD_TAIL - back:SSD_TAIL - back + c]
            conv = conv + shifted * cw_ref[SSD_CONV - 1 - back:SSD_CONV - back, lo:hi]
        prev_ref[:, lo:hi] = x[c - SSD_TAIL:]
        return _silu(conv)

    xs = conv_silu(xs_ref, 0, width)
    bm = conv_silu(sb_ref, width, width + gw).astype(BF16)
    cm = conv_silu(sc_ref, width + gw, width + 2 * gw).astype(BF16)

    dt_in = dt_ref[...] + dtb_ref[...]
    dt = jnp.maximum(dt_in, 0.0) + jnp.log1p(jnp.exp(-jnp.abs(dt_in)))
    b = _cumsum_rows(dt * (-jnp.exp(alog_ref[...])))
    b_t = b.T
    wide = _dot(_split3(jnp.concatenate([dt, b], axis=0), 1), expand_ref[...])
    dt_w, b_w = wide[:c], wide[c:]
    b_last = b_w[c - 1:c, :]
    v = xs * dt_w
    v_bf = v.astype(BF16)
    v_out = (v * jnp.exp(b_last - b_w)).astype(BF16)
    in_scale = jnp.exp(b_w)

    ti = lax.broadcasted_iota(jnp.int32, (c, c), 0)
    si = lax.broadcasted_iota(jnp.int32, (c, c), 1)
    causal = ti >= si
    lane = lax.broadcasted_iota(jnp.int32, (c, 2 * SSD_DIM), 1)
    low = lane < SSD_DIM

    pieces = []
    for g in range(SSD_GROUPS):
        cg = cm[:, g * SSD_STATE:(g + 1) * SSD_STATE]
        bg = bm[:, g * SSD_STATE:(g + 1) * SSD_STATE]
        cb = _dot(cg, bg, NT_DIMS)
        gs = slice(g * per_group * SSD_DIM, (g + 1) * per_group * SSD_DIM)
        state = state_ref[g]
        inter = _dot(cg, state.astype(BF16)) * in_scale[:, gs]
        state_ref[g] = state * jnp.exp(b_last[:, gs]) + _dot(bg, v_out[:, gs], TN_DIMS)
        for pair in range(per_group // 2):
            p_idx = g * (per_group // 2) + pair
            probs = []
            for h in (2 * p_idx, 2 * p_idx + 1):
                w = jnp.exp(jnp.where(causal, b[:, h:h + 1] - b_t[h:h + 1, :], NEG_BIG))
                probs.append((cb * w).astype(BF16))
            v_pair = v_bf[:, p_idx * 2 * SSD_DIM:(p_idx + 1) * 2 * SSD_DIM]
            v_diag = jnp.concatenate([jnp.where(low, v_pair, jnp.zeros_like(v_pair)),
                                      jnp.where(low, jnp.zeros_like(v_pair), v_pair)], axis=0)
            intra = _dot(jnp.concatenate(probs, axis=1), v_diag)
            pieces.append(intra + inter[:, pair * 2 * SSD_DIM:(pair + 1) * 2 * SSD_DIM])
    y = jnp.concatenate(pieces, axis=1) + dskip_ref[...] * xs
    y = y * _silu(z_ref[...])
    o_ref[...] = (_rms(y, 1e-6) * ng_ref[...]).astype(BF16)


def _ssd_expand_matrix():
    width = SSD_HEADS * SSD_DIM
    expand = np.zeros((LANES, width), np.float32)
    for h in range(SSD_HEADS):
        expand[h, h * SSD_DIM:(h + 1) * SSD_DIM] = 1.0
    return np.concatenate([expand] * 3, axis=0)


def _ret_log_gamma(h):
    return math.log(1.0 - 2.0 ** (-5.0 - h))


def _ret_init_mask(mask_ref):
    c = RET_CHUNK
    ti = lax.broadcasted_iota(jnp.int32, (c, c), 0)
    si = lax.broadcasted_iota(jnp.int32, (c, c), 1)
    lag = (ti - si).astype(F32)
    for h in range(RET_HEADS):
        mask_ref[h] = jnp.where(ti >= si, jnp.exp(lag * _ret_log_gamma(h)), 0.0)


def _ret_body(q_ref, k_ref, v_ref, g_ref, o_ref, state_ref, mask_ref):
    c = RET_CHUNK
    pos =lax.broadcasted_iota(jnp.int32, (c, 1), 0).astype(F32)
    for h in range(RET_HEADS):
        lg = _ret_log_gamma(h)
        qh = q_ref[:, h * RET_QK:(h + 1) * RET_QK]
        kh = k_ref[:, h * RET_QK:(h + 1) * RET_QK] * (RET_QK ** -0.5)
        vh = v_ref[:, h * RET_V:(h + 1) * RET_V].astype(BF16)
        scores = _dot(qh.astype(BF16), kh.astype(BF16), NT_DIMS) * mask_ref[h]
        q_in = (qh * jnp.exp((pos + 1.0) * lg)).astype(BF16)
        k_out = (kh * jnp.exp((c - 1.0 - pos) * lg)).astype(BF16)
        state_t = state_ref[h]
        o = _dot(scores.astype(BF16), vh) + _dot(q_in, state_t.astype(BF16), NT_DIMS)
        state_ref[h] = state_t * math.exp(c * lg) + _dot(vh, k_out, TN_DIMS)
        centered = o - jnp.mean(o, axis=-1, keepdims=True)
        normed = centered * lax.rsqrt(jnp.mean(centered * centered, axis=-1, keepdims=True) + 1e-5)
        gh = g_ref[:, h * RET_V:(h + 1) * RET_V]
        o_ref[:, h * RET_V:(h + 1) * RET_V] = (normed * _silu(gh)).astype(BF16)


MIX_ROWS = 128
assert HGRN_CHUNK == ATTN_BLOCK == SSD_CHUNK == RET_CHUNK == MIX_ROWS


def _mixers_kernel(sink_ref,
                   hq_ref, hf_ref, hi_ref, hg_ref, lb_ref, hng_ref, mat_ref,
                   aq_ref, ak_ref, av_ref,
                   xs_ref, sb_ref, sc_ref, sz_ref, dt_ref, cw_ref, cb_ref, dtb_ref, alog_ref,
                   dskip_ref, sng_ref, expand_ref,
                   rq_ref, rk_ref, rv_ref, rg_ref,
                   oh_ref, oa_ref, os_ref, or_ref,
                   hstate_ref, aprev_ref, abias_ref, sprev_ref, sstate_ref, rstate_ref, rmask_ref):
    @pl.when(pl.program_id(0) == 0)
    def _():
        for ref in (hstate_ref, aprev_ref, sprev_ref, sstate_ref, rstate_ref):
            ref[...] = jnp.zeros_like(ref)
        _attn_init_bias(abias_ref)
        _ret_init_mask(rmask_ref)

    _hgrn_body(hq_ref, hf_ref, hi_ref, hg_ref, lb_ref, hng_ref, mat_ref, oh_ref, hstate_ref)
    _attn_body(sink_ref, aq_ref, ak_ref, av_ref, oa_ref, aprev_ref, abias_ref)
    _ssd_body(xs_ref, sb_ref, sc_ref, sz_ref, dt_ref, cw_ref, cb_ref, dtb_ref, alog_ref,
              dskip_ref, sng_ref, expand_ref, os_ref, sprev_ref, sstate_ref)
    _ret_body(rq_ref, rk_ref, rv_ref, rg_ref, or_ref, rstate_ref, rmask_ref)


def _mixers(proj, lower_bound, hgrn_norm_g, sinks, conv_w, conv_b, dt_bias, a_log, d_skip,
            ssd_norm_g):
    t = proj.shape[0]
    c = MIX_ROWS
    bw = BRANCH_WIDTH
    gw = SSD_GROUPS * SSD_STATE
    xw = bw + 2 * gw
    kvw = ATTN_KV_HEADS * ATTN_DIM
    qkw = RET_HEADS * RET_QK
    mat = jnp.asarray(_hgrn_exponent_matrix(), BF16)
    expand = jnp.asarray(_ssd_expand_matrix(), BF16)

    def seg(name, width):
        return pl.BlockSpec((c, width), lambda i, b=COL[name] // width: (i, b))

    def full(a):
        return pl.BlockSpec(a.shape, lambda i: (0, 0))

    def pad_lanes(a):
        return jnp.pad(a.reshape(1, -1), ((0, 0), (0, LANES - a.shape[-1])))

    params = [lower_bound.reshape(1, bw), hgrn_norm_g.reshape(1, bw), mat]
    ssd_params = [conv_w, conv_b.reshape(1, xw), pad_lanes(dt_bias), pad_lanes(a_log),
                  jnp.repeat(d_skip, SSD_DIM).reshape(1, bw), ssd_norm_g.reshape(1, bw), expand]
    in_specs = ([pl.BlockSpec(memory_space=pltpu.SMEM)]
                + [seg(n, bw) for n in ("hq", "hf", "hi", "hg")] + [full(a) for a in params]
                + [seg("aq", bw), seg("ak", kvw), seg("av", kvw)]
                + [seg("xs", bw), seg("sb", gw), seg("sc", gw), seg("sz", bw), seg("dt", LANES)]
                + [full(a) for a in ssd_params]
                + [seg("rq", qkw), seg("rk", qkw), seg("rv", bw), seg("rg", bw)])
    out_spec = pl.BlockSpec((c, bw), lambda i: (i, 0))
    return pl.pallas_call(
        _mixers_kernel,
        grid=(t // c,),
        in_specs=in_specs,
        out_specs=[out_spec] * 4,
        out_shape=[jax.ShapeDtypeStruct((t, bw), BF16)] * 4,
        scratch_shapes=[pltpu.VMEM((HGRN_HEADS, HGRN_DIM, HGRN_DIM), F32),
                        pltpu.VMEM((c, 2 * kvw), F32),
                        pltpu.VMEM((2, ATTN_Q_HEADS, c, 2 * c), F32),
                        pltpu.VMEM((SSD_TAIL, xw), F32),
                        pltpu.VMEM((SSD_GROUPS, SSD_STATE, bw // SSD_GROUPS), F32),
                        pltpu.VMEM((RET_HEADS, RET_V, RET_QK), F32),
                        pltpu.VMEM((RET_HEADS, c, c), F32)],
        compiler_params=_params("arbitrary"),
        name="mixers",
    )(sinks, *([proj] * 4), *params, *([proj] * 3), *([proj] * 5), *ssd_params, *([proj] * 4))


def _w_down_tail(w_down):
    start = (FFN_HIDDEN // FFN_KBLOCK) * FFN_KBLOCK
    return jnp.pad(w_down[:, start:, :], ((0, 0), (0, FFN_PAD - FFN_HIDDEN), (0, 0)))


def kernel(x, norm_mix_pre, norm_mix_post, norm_ffn_pre, norm_ffn_post, w_in,
           hgrn_lb_logits, hgrn_norm, attn_sinks, ssd_conv_w, ssd_conv_b, ssd_dt_bias,
           ssd_a_log, ssd_d, ssd_norm, w_gate_up, b_gate, w_branch, w_out,
           w_ffn_gate, w_ffn_up, w_ffn_down):
    b_, t_, d_ = x.shape
    xr = x.reshape(b_ * t_, d_)
    lower_bounds = _lower_bounds(hgrn_lb_logits)
    w_down_tail = _w_down_tail(w_ffn_down)
    h = _norm(xr, norm_mix_pre[0])
    for l in range(DEPTH):
        proj = _in_proj(h, w_in, l)
        outs = _mixers(proj, lower_bounds[l], hgrn_norm[l], attn_sinks[l], ssd_conv_w[l],
                       ssd_conv_b[l], ssd_dt_bias[l], ssd_a_log[l], ssd_d[l], ssd_norm[l])
        merged = _merge(outs, proj, w_branch, w_gate_up, b_gate, l)
        y = _rows_resident_matmul(merged, w_out, l, D_MODEL // IN_BLOCK, F32, name="out_proj")
        xr, h = _resid_norm(xr, y, norm_mix_post[l], norm_ffn_pre[l])
        u = _gate_up(h, w_ffn_gate, w_ffn_up, l)
        ff = _down(u, w_ffn_down, w_down_tail, l)
        g_next = norm_mix_pre[l + 1] if l + 1 < DEPTH else None
        xr, h = _resid_norm(xr, ff, norm_ffn_post[l], g_next)
    return xr.reshape(b_, t_, d_)
```

```python
import functools
import math

import numpy as np
import jax
import jax.numpy as jnp
from jax import lax
from jax.experimental import pallas as pl
from jax.experimental.pallas import tpu as pltpu

F32 = jnp.float32
BF16 = jnp.bfloat16
HIGHEST = lax.Precision.HIGHEST

D_MODEL = 4096
DEPTH = 4
BRANCH_WIDTH = D_MODEL // 4
HGRN_HEADS, HGRN_DIM, HGRN_CHUNK = 8, 128, 128
HGRN_HALF = HGRN_CHUNK // 2
ATTN_Q_HEADS, ATTN_KV_HEADS, ATTN_DIM, ATTN_BLOCK = 16, 4, 64, 128
SSD_HEADS, SSD_DIM, SSD_GROUPS, SSD_STATE, SSD_CONV, SSD_CHUNK = 16, 64, 4, 128, 4, 128
SSD_TAIL = 8
RET_HEADS, RET_QK, RET_V, RET_CHUNK = 8, 64, 128, 128
GATE_RANK = 256
FFN_HIDDEN = -(-8 * D_MODEL // (3 * 256)) * 256
FFN_BLOCK = 256
FFN_KBLOCK = 1024
FFN_PAD = -(-FFN_HIDDEN // FFN_KBLOCK) * FFN_KBLOCK
IN_WIDTH = 12048

LANES = 128
BF16_ROWS = 16
VMEM_LIMIT = 56 * 1024 * 1024

IN_BLOCK = 256
_FIRST_ORDER = list(range(20)) + list(range(22, 30)) + [20, 21] + list(range(30, 34))
PROJ_SRC = ([9744 + IN_BLOCK * i for i in range(4)] + [10768 + IN_BLOCK * i for i in range(4)]
            + [IN_BLOCK * j for j in _FIRST_ORDER]
            + [8720, 8720 + IN_BLOCK, 9232, 9232 + IN_BLOCK, 11792, 8704])
PROJ_WIDTH = IN_BLOCK * len(PROJ_SRC)
COL = {"rv": 0, "rg": 1024, "hq": 2048, "hf": 3072, "hi": 4096, "hg": 5120, "aq": 6144,
       "sz": 7168, "xs": 8192, "ak": 9216, "av": 9472, "sb": 9728, "sc": 10240,
       "rq": 10752, "rk": 11264, "gc": 11776, "dt": 12032}


NT_DIMS = (((1,), (1,)), ((), ()))
TN_DIMS = (((0,), (0,)), ((), ()))
NEG_BIG = -1e30
LOG2_E = 1.4426950408889634


def _sigmoid(x):
    return 1.0 / (1.0 + jnp.exp2(x * (-LOG2_E)))


def _silu(x):
    return x * _sigmoid(x)


def _rms(x, eps):
    return x * lax.rsqrt(jnp.mean(x * x, axis=-1, keepdims=True) + eps)


def _dot(a, b, dims=None, precision=None):
    if dims is None:
        dims = (((a.ndim - 1,), (0,)), ((), ()))
    return lax.dot_general(a, b, dims, precision=precision, preferred_element_type=F32)


def _split3(x, axis):
    hi = x.astype(BF16)
    rest = x - hi.astype(F32)
    mid = rest.astype(BF16)
    lo = (rest - mid.astype(F32)).astype(BF16)
    return jnp.concatenate([hi, mid, lo], axis=axis)


def _cumsum_rows(x):
    n = x.shape[0]
    row = lax.broadcasted_iota(jnp.int32, x.shape, 0)
    k = 1
    while k < n:
        x = x + jnp.where(row >= k, pltpu.roll(x, k, 0), 0.0)
        k *= 2
    return x


def _params(*sem):
    return pltpu.CompilerParams(dimension_semantics=sem, vmem_limit_bytes=VMEM_LIMIT)


def _rows_resident_kernel(a_ref, w0_ref, w1_ref, o_ref):
    a = a_ref[...]
    tn = w0_ref.shape[1]
    o_ref[:, :tn] = _dot(a, w0_ref[...].astype(BF16)).astype(o_ref.dtype)
    o_ref[:, tn:] = _dot(a, w1_ref[...].astype(BF16)).astype(o_ref.dtype)


def _rows_resident_matmul(a, w, layer, out_dtype, tm=2048, tn=IN_BLOCK, name="matmul"):
    m, kd = a.shape
    n = w.shape[2]
    return pl.pallas_call(
        _rows_resident_kernel,
        grid=(m // tm, n // (2 * tn)),
        in_specs=[pl.BlockSpec((tm, kd), lambda i, j: (i, 0), pipeline_mode=pl.Buffered(1))]
        + [pl.BlockSpec((None, kd, tn), lambda i, j, k=k: (layer, 0, 2 * j + k)) for k in range(2)],
        out_specs=pl.BlockSpec((tm, 2 * tn), lambda i, j: (i, j)),
        out_shape=jax.ShapeDtypeStruct((m, n), out_dtype),
        compiler_params=_params("parallel", "arbitrary"),
        name=name,
    )(a, w, w)


def _in_proj_kernel(src_ref, a_ref, wt0_ref, wt1_ref, o_ref):
    del src_ref
    a = a_ref[...]
    o_ref[:, :IN_BLOCK] = _dot(a, wt0_ref[...].astype(BF16), NT_DIMS)
    o_ref[:, IN_BLOCK:] = _dot(a, wt1_ref[...].astype(BF16), NT_DIMS)


def _in_proj(h, w_in, layer, tm=2048):
    t, d = h.shape
    w_t = jnp.swapaxes(w_in, 1, 2)
    src = jnp.asarray(PROJ_SRC, jnp.int32)
    grid_spec = pltpu.PrefetchScalarGridSpec(
        num_scalar_prefetch=1,
        grid=(t // tm, len(PROJ_SRC) // 2),
        in_specs=[pl.BlockSpec((tm, d), lambda i, p, src: (i, 0), pipeline_mode=pl.Buffered(1))]
        + [pl.BlockSpec((None, pl.Element(IN_BLOCK), pl.Element(d)),
                        lambda i, p, src, k=k: (layer, pl.multiple_of(src[2 * p + k], 8), 0))
           for k in range(2)],
        out_specs=pl.BlockSpec((tm, 2 * IN_BLOCK), lambda i, p, src: (i, p)),
    )
    return pl.pallas_call(
        _in_proj_kernel,
        grid_spec=grid_spec,
        out_shape=jax.ShapeDtypeStruct((t, PROJ_WIDTH), F32),
        compiler_params=_params("parallel", "arbitrary"),
        name="in_proj",
    )(src, h, w_t, w_t)


def _norm_kernel(x_ref, g_ref, h_ref):
    h_ref[...] = (_rms(x_ref[...], 1e-6) * g_ref[...]).astype(BF16)


def _norm(x, g, rows=256):
    t, d = x.shape
    return pl.pallas_call(
        _norm_kernel,
        grid=(t // rows,),
        in_specs=[pl.BlockSpec((rows, d), lambda i: (i, 0)),
                  pl.BlockSpec((1, d), lambda i: (0, 0))],
        out_specs=pl.BlockSpec((rows, d), lambda i: (i, 0)),
        out_shape=jax.ShapeDtypeStruct((t, d), BF16),
        compiler_params=_params("parallel"),
        name="norm",
    )(x, g.reshape(1, d))


def _resid_norm_kernel(x_ref, y_ref, gpost_ref, gnext_ref, xo_ref, h_ref):
    xn = x_ref[...] + _rms(y_ref[...], 1e-6) * gpost_ref[...]
    xo_ref[...] = xn
    h_ref[...] = (_rms(xn, 1e-6) * gnext_ref[...]).astype(BF16)


def _resid_kernel(x_ref, y_ref, gpost_ref, xo_ref):
    xo_ref[...] = x_ref[...] + _rms(y_ref[...], 1e-6) * gpost_ref[...]


def _resid_norm(x, y, g_post, g_next, rows=256):
    t, d = x.shape
    row_spec = pl.BlockSpec((rows, d), lambda i: (i, 0))
    vec_spec = pl.BlockSpec((1, d), lambda i: (0, 0))
    if g_next is None:
        return pl.pallas_call(
            _resid_kernel, grid=(t // rows,),
            in_specs=[row_spec, row_spec, vec_spec],
            out_specs=row_spec,
            out_shape=jax.ShapeDtypeStruct((t, d), F32),
            compiler_params=_params("parallel"),
            name="resid",
        )(x, y, g_post.reshape(1, d)), None
    return pl.pallas_call(
        _resid_norm_kernel, grid=(t // rows,),
        in_specs=[row_spec, row_spec, vec_spec, vec_spec],
        out_specs=[row_spec, row_spec],
        out_shape=[jax.ShapeDtypeStruct((t, d), F32), jax.ShapeDtypeStruct((t, d), BF16)],
        compiler_params=_params("parallel"),
        name="resid_norm",
    )(x, y, g_post.reshape(1, d), g_next.reshape(1, d))


def _merge_kernel(oa_ref, ob_ref, oc_ref, od_ref, gc_ref, wb_ref,
                  wg0_ref, wg1_ref, wg2_ref, wg3_ref, bg0_ref, bg1_ref, bg2_ref, bg3_ref,
                  out_ref):
    gc = gc_ref[...].astype(BF16)
    branches = ((oa_ref, wg0_ref, bg0_ref), (ob_ref, wg1_ref, bg1_ref),
                (oc_ref, wg2_ref, bg2_ref), (od_ref, wg3_ref, bg3_ref))
    acc = None
    for br, (o_ref, wg_ref, bg_ref) in enumerate(branches):
        y = _dot(o_ref[...], wb_ref[br].astype(BF16))
        gate = _sigmoid(_dot(gc, wg_ref[...].astype(BF16)) + bg_ref[...])
        acc = gate * y if acc is None else acc + gate * y
    out_ref[...] = acc.astype(BF16)


def _merge(outs, proj, w_branch, w_gate_up, b_gate, layer, tm=2048, tn=256):
    t = proj.shape[0]
    d = D_MODEL
    nj = d // tn
    once = pl.Buffered(1)
    o_spec = pl.BlockSpec((tm, BRANCH_WIDTH), lambda i, j: (i, 0), pipeline_mode=once)
    gc_spec = pl.BlockSpec((tm, GATE_RANK), lambda i, j: (i, COL["gc"] // GATE_RANK),
                           pipeline_mode=once)
    wb_spec = pl.BlockSpec((None, 4, BRANCH_WIDTH, tn), lambda i, j: (layer, 0, 0, j))
    wg_specs = [pl.BlockSpec((None, GATE_RANK, tn), lambda i, j, br=br: (layer, 0, br * nj + j))
                for br in range(4)]
    bg_specs = [pl.BlockSpec((None, 1, tn), lambda i, j, br=br: (layer, 0, br * nj + j))
                for br in range(4)]
    bg = b_gate.reshape(DEPTH, 1, 4 * d)
    return pl.pallas_call(
        _merge_kernel,
        grid=(t // tm, nj),
        in_specs=[o_spec] * 4 + [gc_spec, wb_spec] + wg_specs + bg_specs,
        out_specs=pl.BlockSpec((tm, tn), lambda i, j: (i, j)),
        out_shape=jax.ShapeDtypeStruct((t, d), BF16),
        compiler_params=_params("parallel", "arbitrary"),
        name="merge",
    )(*outs, proj, w_branch, *([w_gate_up] * 4), *([bg] * 4))


def _gate_up_kernel(h_ref, wg_ref, wu_ref, o_ref, *, n_real):
    j = pl.program_id(1)

    @pl.when(j < n_real)
    def _():
        h = h_ref[...]
        gate = _dot(h, wg_ref[...].astype(BF16))
        up = _dot(h, wu_ref[...].astype(BF16))
        o_ref[...] = (_silu(gate) * up).astype(BF16)

    @pl.when(j >= n_real)
    def _():
        o_ref[...] = jnp.zeros_like(o_ref)


def _gate_up(h, w_gate, w_up, layer, tm=2048, tn=FFN_BLOCK):
    t, d = h.shape
    n_real = FFN_HIDDEN // tn
    w_spec = pl.BlockSpec((None, d, tn), lambda i, j: (layer, 0, jnp.minimum(j, n_real - 1)))
    return pl.pallas_call(
        functools.partial(_gate_up_kernel, n_real=n_real),
        grid=(t // tm, FFN_PAD // tn),
        in_specs=[pl.BlockSpec((tm, d), lambda i, j: (i, 0), pipeline_mode=pl.Buffered(1)),
                  w_spec, w_spec],
        out_specs=pl.BlockSpec((tm, tn), lambda i, j: (i, j)),
        out_shape=jax.ShapeDtypeStruct((t, FFN_PAD), BF16),
        compiler_params=_params("parallel", "arbitrary"),
        name="ffn_gate_up",
    )(h, w_gate, w_up)


def _down_kernel(u_ref, w_ref, wt_ref, o_ref, *, n_main):
    k = pl.program_id(2)
    u = u_ref[...]

    @pl.when(k == 0)
    def _():
        o_ref[...] = _dot(u, w_ref[...].astype(BF16))

    @pl.when((k > 0) & (k < n_main))
    def _():
        o_ref[...] += _dot(u, w_ref[...].astype(BF16))

    @pl.when(k == n_main)
    def _():
        o_ref[...] += _dot(u, wt_ref[...].astype(BF16))


def _down(u, w_down, w_down_tail, layer, tm=2048, tn=1024, tk=FFN_KBLOCK):
    t = u.shape[0]
    d = w_down.shape[2]
    n_main = FFN_HIDDEN // tk
    return pl.pallas_call(
        functools.partial(_down_kernel, n_main=n_main),
        grid=(t // tm, d // tn, FFN_PAD // tk),
        in_specs=[pl.BlockSpec((tm, tk), lambda i, j, k: (i, k)),
                  pl.BlockSpec((None, tk, tn), lambda i, j, k: (layer, jnp.minimum(k, n_main - 1), j)),
                  pl.BlockSpec((None, tk, tn), lambda i, j, k: (layer, 0, j))],
        out_specs=pl.BlockSpec((tm, tn), lambda i, j, k: (i, j)),
        out_shape=jax.ShapeDtypeStruct((t, d), F32),
        compiler_params=_params("parallel", "parallel", "arbitrary"),
        name="ffn_down",
    )(u, w_down, w_down_tail)


def _lower_bound_kernel(l_ref, o_ref):
    lg = l_ref[...]
    e = jnp.exp(lg - jnp.max(lg, axis=0, keepdims=True))
    p = e / jnp.sum(e, axis=0, keepdims=True)
    run = p[0:1]
    o_ref[0:1, :] = run - p[0:1]
    for l in range(1, DEPTH):
        run = run + p[l:l + 1]
        o_ref[l:l + 1, :] = run - p[0:1]


def _lower_bounds(logits):
    return pl.pallas_call(
        _lower_bound_kernel,
        out_shape=jax.ShapeDtypeStruct(logits.shape, F32),
        name="hgrn_lower_bounds",
    )(logits)


HGRN_LEVELS = (32, 16, 8, 4, 2, 1)


def _hgrn_exponent_matrix():
    c = HGRN_HALF
    mat = np.zeros(((len(HGRN_LEVELS) + 2) * c, c), np.float32)
    for li, m in enumerate(HGRN_LEVELS):
        for t in range(c):
            r = (t // (2 * m)) * 2 * m + m - 1
            if (t // m) % 2 == 1:
                mat[li * c + t, r + 1:t + 1] = 1.0
            else:
                mat[li * c + t, t + 1:r + 1] = 1.0
    for t in range(c):
        mat[len(HGRN_LEVELS) * c + t, :t + 1] = 1.0
        mat[(len(HGRN_LEVELS) + 1) * c + t, t + 1:] = 1.0
    return np.concatenate([mat, mat, mat], axis=1)


HGRN_SIZES = (HGRN_HALF,) + HGRN_LEVELS
HGRN_COARSE = sum(1 for m in HGRN_SIZES if m % BF16_ROWS == 0)


def _hgrn_init_masks(smask_ref, rmask_ref):
    c = HGRN_CHUNK
    row = lax.broadcasted_iota(jnp.int32, (c, c), 0)
    col = lax.broadcasted_iota(jnp.int32, (c, c), 1)
    for li, m in enumerate(HGRN_SIZES):
        shift = m.bit_length()
        smask_ref[li] = jnp.where((row >> shift) == (col >> shift), 1.0, 0.0)
    smask_ref[len(HGRN_SIZES)] = jnp.where(row == col, 1.0, 0.0)
    rows = lax.broadcasted_iota(jnp.int32, (c, HGRN_DIM), 0)
    for j, m in enumerate(HGRN_SIZES[HGRN_COARSE:]):
        upper = ((rows >> (m.bit_length() - 1)) & 1) == 1
        rmask_ref[0, j] = jnp.where(upper, 1.0, 0.0).astype(BF16)
        rmask_ref[1, j] = jnp.where(upper, 0.0, 1.0).astype(BF16)


def _hgrn_body(q_ref, f_ref, i_ref, g_ref, lb_ref, ng_ref, mat_ref, o_ref, state_ref,
               smask_ref, rmask_ref):
    c = HGRN_CHUNK
    hc = HGRN_HALF
    nl = len(HGRN_LEVELS)

    lb = lb_ref[...]
    forget = lb + (1.0 - lb) * _sigmoid(f_ref[...])
    key = 1.0 - forget
    query = _silu(q_ref[...])
    log_f = jnp.log(forget)
    sums = [_dot(mat_ref[...], _split3(log_f[s * hc:(s + 1) * hc], 0)) for s in range(2)]

    def both(group):
        return jnp.concatenate([sm[group * hc:(group + 1) * hc] for sm in sums], axis=0)

    prefix = [sm[nl * hc:(nl + 1) * hc] for sm in sums]
    suffix = [sm[(nl + 1) * hc:(nl + 2) * hc] for sm in sums]
    total = [p[hc - 1:hc] for p in prefix]
    level_exp = [jnp.concatenate([suffix[0], prefix[1]], axis=0)] + [both(li) for li in range(nl)]
    b_in = jnp.exp(jnp.concatenate([prefix[0], prefix[1] + total[0]], axis=0))
    b_out = jnp.exp(jnp.concatenate([suffix[0] + total[1], suffix[1]], axis=0))
    chunk_decay = jnp.exp(total[0] + total[1])

    for h in range(HGRN_HEADS):
        sl = slice(h * HGRN_DIM, (h + 1) * HGRN_DIM)
        qh, kh = query[:, sl], key[:, sl]
        qb, kb = qh.astype(BF16), kh.astype(BF16)
        vh = i_ref[:, sl].astype(BF16)
        scores = _dot(qb, kb, NT_DIMS) * smask_ref[len(HGRN_SIZES)]
        for li, m in enumerate(HGRN_SIZES):
            dl = jnp.exp(level_exp[li][:, sl]).astype(BF16)
            if m % BF16_ROWS == 0:
                zero = jnp.zeros((m, HGRN_DIM), BF16)
                q_parts, k_parts = [], []
                for blk in range(c // m):
                    rs = slice(blk * m, (blk + 1) * m)
                    is_upper = blk % 2 == 1
                    q_parts.append(qb[rs] * dl[rs] if is_upper else zero)
                    k_parts.append(zero if is_upper else kb[rs] * dl[rs])
                qt = jnp.concatenate(q_parts, axis=0)
                kt = jnp.concatenate(k_parts, axis=0)
            else:
                fine = li - HGRN_COARSE
                qt = qb * dl * rmask_ref[0, fine]
                kt = kb * dl * rmask_ref[1, fine]
            level = _dot(qt, kt, NT_DIMS)
            scores = scores + (level if 2 * m == c else level * smask_ref[li])
        q_in = (qh * b_in[:, sl]).astype(BF16)
        k_out = (kh * b_out[:, sl]).astype(BF16)
        state_t = state_ref[h]
        o = _dot(scores.astype(BF16), vh) + _dot(q_in, state_t.astype(BF16), NT_DIMS)
        state_ref[h] = state_t * chunk_decay[:, sl] + _dot(vh, k_out, TN_DIMS)
        gh = g_ref[:, sl]
        o_ref[:, sl] = (_rms(o, 1e-6) * ng_ref[:, sl] * _silu(gh)).astype(BF16)


def _attn_slope(h):
    return 2.0 ** (-8.0 * (h + 1) / ATTN_Q_HEADS)


def _attn_init_bias(bias_ref):
    c = ATTN_BLOCK
    qi = lax.broadcasted_iota(jnp.int32, (c, 2 * c), 0)
    kj = lax.broadcasted_iota(jnp.int32, (c, 2 * c), 1)
    dist = qi + c - kj
    in_window = (dist >= 0) & (dist < c)
    dist_f = dist.astype(F32)
    for h in range(ATTN_Q_HEADS):
        alibi = -_attn_slope(h) * dist_f
        bias_ref[0, h] = jnp.where(in_window & (kj >= c), alibi, NEG_BIG)
        bias_ref[1, h] = jnp.where(in_window, alibi, NEG_BIG)


def _attn_body(sink_ref, q_ref, k_ref, v_ref, o_ref, prev_ref, bias_ref):
    c = ATTN_BLOCK
    kvw = ATTN_KV_HEADS * ATTN_DIM
    which = jnp.where(pl.program_id(0) == 0, 0, 1)

    k_all =jnp.concatenate([prev_ref[:, :kvw], k_ref[...]], axis=0).astype(BF16)
    v_all = jnp.concatenate([prev_ref[:, kvw:], v_ref[...]], axis=0).astype(BF16)
    group = ATTN_Q_HEADS // ATTN_KV_HEADS
    outs = []
    for h in range(ATTN_Q_HEADS):
        g = h // group
        qh = (q_ref[:, h * ATTN_DIM:(h + 1) * ATTN_DIM] * (ATTN_DIM ** -0.5)).astype(BF16)
        kg = k_all[:, g * ATTN_DIM:(g + 1) * ATTN_DIM]
        vg = v_all[:, g * ATTN_DIM:(g + 1) * ATTN_DIM]
        logits = _dot(qh, kg, NT_DIMS) + bias_ref[which, h]
        sink = sink_ref[h]
        mx = jnp.maximum(jnp.max(logits, axis=-1, keepdims=True), sink)
        p = jnp.exp(logits - mx)
        denom = jnp.sum(p, axis=-1, keepdims=True) + jnp.exp(sink - mx)
        outs.append(_dot(p.astype(BF16), vg) / denom)
    o_ref[...] = jnp.concatenate(outs, axis=1).astype(BF16)
    prev_ref[:, :kvw] = k_ref[...]
    prev_ref[:, kvw:] = v_ref[...]


def _ssd_body(xs_ref, sb_ref, sc_ref, z_ref, dt_ref, cw_ref, cb_ref, dtb_ref, alog_ref,
              dskip_ref, ng_ref, expand_ref, o_ref, prev_ref, state_ref):
    c = SSD_CHUNK
    width = SSD_HEADS * SSD_DIM
    gw = SSD_GROUPS * SSD_STATE
    per_group = SSD_HEADS // SSD_GROUPS

    def conv_silu(x_ref, lo, hi):
        x = x_ref[...]
        ext = jnp.concatenate([prev_ref[:, lo:hi], x], axis=0)
        conv = x * cw_ref[SSD_CONV - 1:SSD_CONV, lo:hi] + cb_ref[:, lo:hi]
        for back in range(1, SSD_CONV):
            shifted = ext[SSD_TAIL - back:SSD_TAIL - back + c]
            conv = conv + shifted * cw_ref[SSD_CONV - 1 - back:SSD_CONV - back, lo:hi]
        prev_ref[:, lo:hi] = x[c - SSD_TAIL:]
        return _silu(conv)

    xs = conv_silu(xs_ref, 0, width)
    bm = conv_silu(sb_ref, width, width + gw).astype(BF16)
    cm = conv_silu(sc_ref, width + gw, width + 2 * gw).astype(BF16)

    dt_in = dt_ref[...] + dtb_ref[...]
    dt = jnp.maximum(dt_in, 0.0) + jnp.log1p(jnp.exp(-jnp.abs(dt_in)))
    b = _cumsum_rows(dt * (-jnp.exp(alog_ref[...])))
    b_t = b.T
    wide = _dot(_split3(jnp.concatenate([dt, b], axis=0), 1), expand_ref[...])
    dt_w, b_w = wide[:c], wide[c:]
    b_last = b_w[c - 1:c, :]
    v = xs * dt_w
    v_bf = v.astype(BF16)
    v_out = (v * jnp.exp(b_last - b_w)).astype(BF16)
    in_scale = jnp.exp(b_w)

    ti = lax.broadcasted_iota(jnp.int32, (c, c), 0)
    si = lax.broadcasted_iota(jnp.int32, (c, c), 1)
    causal = ti >= si
    lane = lax.broadcasted_iota(jnp.int32, (c, 2 * SSD_DIM), 1)
    low = lane < SSD_DIM

    pieces = []
    for g in range(SSD_GROUPS):
        cg = cm[:, g * SSD_STATE:(g + 1) * SSD_STATE]
        bg = bm[:, g * SSD_STATE:(g + 1) * SSD_STATE]
        cb = _dot(cg, bg, NT_DIMS)
        gs = slice(g * per_group * SSD_DIM, (g + 1) * per_group * SSD_DIM)
        state = state_ref[g]
        inter = _dot(cg, state.astype(BF16)) * in_scale[:, gs]
        state_ref[g] = state * jnp.exp(b_last[:, gs]) + _dot(bg, v_out[:, gs], TN_DIMS)
        for pair in range(per_group // 2):
            p_idx = g * (per_group // 2) + pair
            probs = []
            for h in (2 * p_idx, 2 * p_idx + 1):
                w = jnp.exp(jnp.where(causal, b[:, h:h + 1] - b_t[h:h + 1, :], NEG_BIG))
                probs.append((cb * w).astype(BF16))
            v_pair = v_bf[:, p_idx * 2 * SSD_DIM:(p_idx + 1) * 2 * SSD_DIM]
            v_diag = jnp.concatenate([jnp.where(low, v_pair, jnp.zeros_like(v_pair)),
                                      jnp.where(low, jnp.zeros_like(v_pair), v_pair)], axis=0)
            intra = _dot(jnp.concatenate(probs, axis=1), v_diag)
            pieces.append(intra + inter[:, pair * 2 * SSD_DIM:(pair + 1) * 2 * SSD_DIM])
    y = jnp.concatenate(pieces, axis=1) + dskip_ref[...] * xs
    y = y * _silu(z_ref[...])
    o_ref[...] = (_rms(y, 1e-6) * ng_ref[...]).astype(BF16)


def _ssd_expand_matrix():
    width = SSD_HEADS * SSD_DIM
    expand = np.zeros((LANES, width), np.float32)
    for h in range(SSD_HEADS):
        expand[h, h * SSD_DIM:(h + 1) * SSD_DIM] = 1.0
    return np.concatenate([expand] * 3, axis=0)


def _ret_log_gamma(h):
    return math.log(1.0 - 2.0 ** (-5.0 - h))


def _ret_init_mask(mask_ref):
    c = RET_CHUNK
    ti = lax.broadcasted_iota(jnp.int32, (c, c), 0)
    si = lax.broadcasted_iota(jnp.int32, (c, c), 1)
    lag = (ti - si).astype(F32)
    for h in range(RET_HEADS):
        mask_ref[h] = jnp.where(ti >= si, jnp.exp(lag * _ret_log_gamma(h)), 0.0)


def _ret_body(q_ref, k_ref, v_ref, g_ref, o_ref, state_ref, mask_ref):
    c = RET_CHUNK
    pos =lax.broadcasted_iota(jnp.int32, (c, 1), 0).astype(F32)
    for h in range(RET_HEADS):
        lg = _ret_log_gamma(h)
        qh = q_ref[:, h * RET_QK:(h + 1) * RET_QK]
        kh = k_ref[:, h * RET_QK:(h + 1) * RET_QK] * (RET_QK ** -0.5)
        vh = v_ref[:, h * RET_V:(h + 1) * RET_V].astype(BF16)
        scores = _dot(qh.astype(BF16), kh.astype(BF16), NT_DIMS) * mask_ref[h]
        q_in = (qh * jnp.exp((pos + 1.0) * lg)).astype(BF16)
        k_out = (kh * jnp.exp((c - 1.0 - pos) * lg)).astype(BF16)
        state_t = state_ref[h]
        o = _dot(scores.astype(BF16), vh) + _dot(q_in, state_t.astype(BF16), NT_DIMS)
        state_ref[h] = state_t * math.exp(c * lg) + _dot(vh, k_out, TN_DIMS)
        centered = o - jnp.mean(o, axis=-1, keepdims=True)
        normed = centered * lax.rsqrt(jnp.mean(centered * centered, axis=-1, keepdims=True) + 1e-5)
        gh = g_ref[:, h * RET_V:(h + 1) * RET_V]
        o_ref[:, h * RET_V:(h + 1) * RET_V] = (normed * _silu(gh)).astype(BF16)


MIX_ROWS = 128
assert HGRN_CHUNK == ATTN_BLOCK == SSD_CHUNK == RET_CHUNK == MIX_ROWS


def _mixers_kernel(sink_ref,
                   hq_ref, hf_ref, hi_ref, hg_ref, lb_ref, hng_ref, mat_ref,
                   aq_ref, ak_ref, av_ref,
                   xs_ref, sb_ref, sc_ref, sz_ref, dt_ref, cw_ref, cb_ref, dtb_ref, alog_ref,
                   dskip_ref, sng_ref, expand_ref,
                   rq_ref, rk_ref, rv_ref, rg_ref,
                   oh_ref, oa_ref, os_ref, or_ref,
                   hstate_ref, hsmask_ref, hrmask_ref, aprev_ref, abias_ref, sprev_ref, sstate_ref,
                   rstate_ref, rmask_ref):
    @pl.when(pl.program_id(0) == 0)
    def _():
        for ref in (hstate_ref, aprev_ref, sprev_ref, sstate_ref, rstate_ref):
            ref[...] = jnp.zeros_like(ref)
        _hgrn_init_masks(hsmask_ref, hrmask_ref)
        _attn_init_bias(abias_ref)
        _ret_init_mask(rmask_ref)

    _hgrn_body(hq_ref, hf_ref, hi_ref, hg_ref, lb_ref, hng_ref, mat_ref, oh_ref, hstate_ref,
               hsmask_ref, hrmask_ref)
    _attn_body(sink_ref, aq_ref, ak_ref, av_ref, oa_ref, aprev_ref, abias_ref)
    _ssd_body(xs_ref, sb_ref, sc_ref, sz_ref, dt_ref, cw_ref, cb_ref, dtb_ref, alog_ref,
              dskip_ref, sng_ref, expand_ref, os_ref, sprev_ref, sstate_ref)
    _ret_body(rq_ref, rk_ref, rv_ref, rg_ref, or_ref, rstate_ref, rmask_ref)


def _mixers(proj, lower_bound, hgrn_norm_g, sinks, conv_w, conv_b, dt_bias, a_log, d_skip,
            ssd_norm_g):
    t = proj.shape[0]
    c = MIX_ROWS
    bw = BRANCH_WIDTH
    gw = SSD_GROUPS * SSD_STATE
    xw = bw + 2 * gw
    kvw = ATTN_KV_HEADS * ATTN_DIM
    qkw = RET_HEADS * RET_QK
    mat = jnp.asarray(_hgrn_exponent_matrix(), BF16)
    expand = jnp.asarray(_ssd_expand_matrix(), BF16)

    def seg(name, width):
        return pl.BlockSpec((c, width), lambda i, b=COL[name] // width: (i, b))

    def full(a):
        return pl.BlockSpec(a.shape, lambda i: (0, 0))

    def pad_lanes(a):
        return jnp.pad(a.reshape(1, -1), ((0, 0), (0, LANES - a.shape[-1])))

    params = [lower_bound.reshape(1, bw), hgrn_norm_g.reshape(1, bw), mat]
    ssd_params = [conv_w, conv_b.reshape(1, xw), pad_lanes(dt_bias), pad_lanes(a_log),
                  jnp.repeat(d_skip, SSD_DIM).reshape(1, bw), ssd_norm_g.reshape(1, bw), expand]
    in_specs = ([pl.BlockSpec(memory_space=pltpu.SMEM)]
                + [seg(n, bw) for n in ("hq", "hf", "hi", "hg")] + [full(a) for a in params]
                + [seg("aq", bw), seg("ak", kvw), seg("av", kvw)]
                + [seg("xs", bw), seg("sb", gw), seg("sc", gw), seg("sz", bw), seg("dt", LANES)]
                + [full(a) for a in ssd_params]
                + [seg("rq", qkw), seg("rk", qkw), seg("rv", bw), seg("rg", bw)])
    out_spec = pl.BlockSpec((c, bw), lambda i: (i, 0))
    return pl.pallas_call(
        _mixers_kernel,
        grid=(t // c,),
        in_specs=in_specs,
        out_specs=[out_spec] * 4,
        out_shape=[jax.ShapeDtypeStruct((t, bw), BF16)] * 4,
        scratch_shapes=[pltpu.VMEM((HGRN_HEADS, HGRN_DIM, HGRN_DIM), F32),
                        pltpu.VMEM((len(HGRN_SIZES) + 1, c, c), F32),
                        pltpu.VMEM((2, len(HGRN_SIZES) - HGRN_COARSE, c, HGRN_DIM), BF16),
                        pltpu.VMEM((c, 2 * kvw), F32),
                        pltpu.VMEM((2, ATTN_Q_HEADS, c, 2 * c), F32),
                        pltpu.VMEM((SSD_TAIL, xw), F32),
                        pltpu.VMEM((SSD_GROUPS, SSD_STATE, bw // SSD_GROUPS), F32),
                        pltpu.VMEM((RET_HEADS, RET_V, RET_QK), F32),
                        pltpu.VMEM((RET_HEADS, c, c), F32)],
        compiler_params=_params("arbitrary"),
        name="mixers",
    )(sinks, *([proj] * 4), *params, *([proj] * 3), *([proj] * 5), *ssd_params, *([proj] * 4))


def _w_down_tail(w_down):
    start = (FFN_HIDDEN // FFN_KBLOCK) * FFN_KBLOCK
    return jnp.pad(w_down[:, start:, :], ((0, 0), (0, FFN_PAD - FFN_HIDDEN), (0, 0)))


def kernel(x, norm_mix_pre, norm_mix_post, norm_ffn_pre, norm_ffn_post, w_in,
           hgrn_lb_logits, hgrn_norm, attn_sinks, ssd_conv_w, ssd_conv_b, ssd_dt_bias,
           ssd_a_log, ssd_d, ssd_norm, w_gate_up, b_gate, w_branch, w_out,
           w_ffn_gate, w_ffn_up, w_ffn_down):
    b_, t_, d_ = x.shape
    xr = x.reshape(b_ * t_, d_)
    lower_bounds = _lower_bounds(hgrn_lb_logits)
    w_down_tail = _w_down_tail(w_ffn_down)
    h = _norm(xr, norm_mix_pre[0])
    for l in range(DEPTH):
        proj = _in_proj(h, w_in, l)
        outs = _mixers(proj, lower_bounds[l], hgrn_norm[l], attn_sinks[l], ssd_conv_w[l],
                       ssd_conv_b[l], ssd_dt_bias[l], ssd_a_log[l], ssd_d[l], ssd_norm[l])
        merged = _merge(outs, proj, w_branch, w_gate_up, b_gate, l)
        y = _rows_resident_matmul(merged, w_out, l, F32, name="out_proj")
        xr, h = _resid_norm(xr, y, norm_mix_post[l], norm_ffn_pre[l])
        u = _gate_up(h, w_ffn_gate, w_ffn_up, l)
        ff = _down(u, w_ffn_down, w_down_tail, l)
        g_next = norm_mix_pre[l + 1] if l + 1 < DEPTH else None
        xr, h = _resid_norm(xr, ff, norm_ffn_post[l], g_next)
    return xr.reshape(b_, t_, d_)
```

```python
import functools
import math

import numpy as np
import jax
import jax.numpy as jnp
from jax import lax
from jax.experimental import pallas as pl
from jax.experimental.pallas import tpu as pltpu

F32 = jnp.float32
BF16 = jnp.bfloat16
HIGHEST = lax.Precision.HIGHEST

D_MODEL = 4096
DEPTH = 4
BRANCH_WIDTH = D_MODEL // 4
HGRN_HEADS, HGRN_DIM, HGRN_CHUNK = 8, 128, 128
HGRN_HALF = HGRN_CHUNK // 2
ATTN_Q_HEADS, ATTN_KV_HEADS, ATTN_DIM, ATTN_BLOCK = 16, 4, 64, 128
SSD_HEADS, SSD_DIM, SSD_GROUPS, SSD_STATE, SSD_CONV, SSD_CHUNK = 16, 64, 4, 128, 4, 128
SSD_TAIL = 8
RET_HEADS, RET_QK, RET_V, RET_CHUNK = 8, 64, 128, 128
GATE_RANK = 256
FFN_HIDDEN = -(-8 * D_MODEL // (3 * 256)) * 256
FFN_BLOCK = 256
FFN_KBLOCK = 1024
FFN_PAD = -(-FFN_HIDDEN // FFN_KBLOCK) * FFN_KBLOCK
IN_WIDTH = 12048

LANES = 128
BF16_ROWS = 16
VMEM_LIMIT = 56 * 1024 * 1024

IN_BLOCK = 256
_FIRST_ORDER = list(range(20)) + list(range(22, 30)) + [20, 21] + list(range(30, 34))
PROJ_SRC = ([9744 + IN_BLOCK * i for i in range(4)] + [10768 + IN_BLOCK * i for i in range(4)]
            + [IN_BLOCK * j for j in _FIRST_ORDER]
            + [8720, 8720 + IN_BLOCK, 9232, 9232 + IN_BLOCK, 11792, 8704])
PROJ_WIDTH = IN_BLOCK * len(PROJ_SRC)
COL = {"rv": 0, "rg": 1024, "hq": 2048, "hf": 3072, "hi": 4096, "hg": 5120, "aq": 6144,
       "sz": 7168, "xs": 8192, "ak": 9216, "av": 9472, "sb": 9728, "sc": 10240,
       "rq": 10752, "rk": 11264, "gc": 11776, "dt": 12032}


NT_DIMS = (((1,), (1,)), ((), ()))
TN_DIMS = (((0,), (0,)), ((), ()))
NEG_BIG = -1e30
LOG2_E = 1.4426950408889634


def _sigmoid(x):
    return 1.0 / (1.0 + jnp.exp2(x * (-LOG2_E)))


def _silu(x):
    return x * _sigmoid(x)


def _rms(x, eps):
    return x * lax.rsqrt(jnp.mean(x * x, axis=-1, keepdims=True) + eps)


def _dot(a, b, dims=None, precision=None):
    if dims is None:
        dims = (((a.ndim - 1,), (0,)), ((), ()))
    return lax.dot_general(a, b, dims, precision=precision, preferred_element_type=F32)


def _split3(x, axis):
    hi = x.astype(BF16)
    rest = x - hi.astype(F32)
    mid = rest.astype(BF16)
    lo = (rest - mid.astype(F32)).astype(BF16)
    return jnp.concatenate([hi, mid, lo], axis=axis)


def _cumsum_rows(x):
    n = x.shape[0]
    row = lax.broadcasted_iota(jnp.int32, x.shape, 0)
    k = 1
    while k < n:
        x = x + jnp.where(row >= k, pltpu.roll(x, k, 0), 0.0)
        k *= 2
    return x


def _params(*sem):
    return pltpu.CompilerParams(dimension_semantics=sem, vmem_limit_bytes=VMEM_LIMIT)


def _rows_resident_kernel(a_ref, w0_ref, w1_ref, o_ref):
    a = a_ref[...]
    tn = w0_ref.shape[1]
    o_ref[:, :tn] = _dot(a, w0_ref[...].astype(BF16)).astype(o_ref.dtype)
    o_ref[:, tn:] = _dot(a, w1_ref[...].astype(BF16)).astype(o_ref.dtype)


def _rows_resident_matmul(a, w, layer, out_dtype, tm=2048, tn=IN_BLOCK, name="matmul"):
    m, kd = a.shape
    n = w.shape[2]
    return pl.pallas_call(
        _rows_resident_kernel,
        grid=(m // tm, n // (2 * tn)),
        in_specs=[pl.BlockSpec((tm, kd), lambda i, j: (i, 0), pipeline_mode=pl.Buffered(1))]
        + [pl.BlockSpec((None, kd, tn), lambda i, j, k=k: (layer, 0, 2 * j + k)) for k in range(2)],
        out_specs=pl.BlockSpec((tm, 2 * tn), lambda i, j: (i, j)),
        out_shape=jax.ShapeDtypeStruct((m, n), out_dtype),
        compiler_params=_params("parallel", "arbitrary"),
        name=name,
    )(a, w, w)


def _in_proj_kernel(src_ref, a_ref, wt0_ref, wt1_ref, o_ref):
    del src_ref
    a = a_ref[...]
    o_ref[:, :IN_BLOCK] = _dot(a, wt0_ref[...].astype(BF16), NT_DIMS)
    o_ref[:, IN_BLOCK:] = _dot(a, wt1_ref[...].astype(BF16), NT_DIMS)


def _in_proj(h, w_in, layer, tm=2048):
    t, d = h.shape
    w_t = jnp.swapaxes(w_in, 1, 2)
    src = jnp.asarray(PROJ_SRC, jnp.int32)
    grid_spec = pltpu.PrefetchScalarGridSpec(
        num_scalar_prefetch=1,
        grid=(t // tm, len(PROJ_SRC) // 2),
        in_specs=[pl.BlockSpec((tm, d), lambda i, p, src: (i, 0), pipeline_mode=pl.Buffered(1))]
        + [pl.BlockSpec((None, pl.Element(IN_BLOCK), pl.Element(d)),
                        lambda i, p, src, k=k: (layer, pl.multiple_of(src[2 * p + k], 8), 0))
           for k in range(2)],
        out_specs=pl.BlockSpec((tm, 2 * IN_BLOCK), lambda i, p, src: (i, p)),
    )
    return pl.pallas_call(
        _in_proj_kernel,
        grid_spec=grid_spec,
        out_shape=jax.ShapeDtypeStruct((t, PROJ_WIDTH), F32),
        compiler_params=_params("parallel", "arbitrary"),
        name="in_proj",
    )(src, h, w_t, w_t)


def _norm_kernel(x_ref, g_ref, h_ref):
    h_ref[...] = (_rms(x_ref[...], 1e-6) * g_ref[...]).astype(BF16)


def _norm(x, g, rows=256):
    t, d = x.shape
    return pl.pallas_call(
        _norm_kernel,
        grid=(t // rows,),
        in_specs=[pl.BlockSpec((rows, d), lambda i: (i, 0)),
                  pl.BlockSpec((1, d), lambda i: (0, 0))],
        out_specs=pl.BlockSpec((rows, d), lambda i: (i, 0)),
        out_shape=jax.ShapeDtypeStruct((t, d), BF16),
        compiler_params=_params("parallel"),
        name="norm",
    )(x, g.reshape(1, d))


def _resid_norm_kernel(x_ref, y_ref, gpost_ref, gnext_ref, xo_ref, h_ref):
    xn = x_ref[...] + _rms(y_ref[...], 1e-6) * gpost_ref[...]
    xo_ref[...] = xn
    h_ref[...] = (_rms(xn, 1e-6) * gnext_ref[...]).astype(BF16)


def _resid_kernel(x_ref, y_ref, gpost_ref, xo_ref):
    xo_ref[...] = x_ref[...] + _rms(y_ref[...], 1e-6) * gpost_ref[...]


def _resid_norm(x, y, g_post, g_next, rows=256):
    t, d = x.shape
    row_spec = pl.BlockSpec((rows, d), lambda i: (i, 0))
    vec_spec = pl.BlockSpec((1, d), lambda i: (0, 0))
    if g_next is None:
        return pl.pallas_call(
            _resid_kernel, grid=(t // rows,),
            in_specs=[row_spec, row_spec, vec_spec],
            out_specs=row_spec,
            out_shape=jax.ShapeDtypeStruct((t, d), F32),
            compiler_params=_params("parallel"),
            name="resid",
        )(x, y, g_post.reshape(1, d)), None
    return pl.pallas_call(
        _resid_norm_kernel, grid=(t // rows,),
        in_specs=[row_spec, row_spec, vec_spec, vec_spec],
        out_specs=[row_spec, row_spec],
        out_shape=[jax.ShapeDtypeStruct((t, d), F32), jax.ShapeDtypeStruct((t, d), BF16)],
        compiler_params=_params("parallel"),
        name="resid_norm",
    )(x, y, g_post.reshape(1, d), g_next.reshape(1, d))


def _merge_kernel(oa_ref, ob_ref, oc_ref, od_ref, gc_ref, wb_ref,
                  wg0_ref, wg1_ref, wg2_ref, wg3_ref, bg0_ref, bg1_ref, bg2_ref, bg3_ref,
                  out_ref):
    gc = gc_ref[...].astype(BF16)
    branches = ((oa_ref, wg0_ref, bg0_ref), (ob_ref, wg1_ref, bg1_ref),
                (oc_ref, wg2_ref, bg2_ref), (od_ref, wg3_ref, bg3_ref))
    acc = None
    for br, (o_ref, wg_ref, bg_ref) in enumerate(branches):
        y = _dot(o_ref[...], wb_ref[br].astype(BF16))
        gate = _sigmoid(_dot(gc, wg_ref[...].astype(BF16)) + bg_ref[...])
        acc = gate * y if acc is None else acc + gate * y
    out_ref[...] = acc.astype(BF16)


def _merge(outs, proj, w_branch, w_gate_up, b_gate, layer, tm=2048, tn=256):
    t = proj.shape[0]
    d = D_MODEL
    nj = d // tn
    once = pl.Buffered(1)
    o_spec = pl.BlockSpec((tm, BRANCH_WIDTH), lambda i, j: (i, 0))
    gc_spec = pl.BlockSpec((tm, GATE_RANK), lambda i, j: (i, COL["gc"] // GATE_RANK),
                           pipeline_mode=once)
    wb_spec = pl.BlockSpec((None, 4, BRANCH_WIDTH, tn), lambda i, j: (layer, 0, 0, j))
    wg_specs = [pl.BlockSpec((None, GATE_RANK, tn), lambda i, j, br=br: (layer, 0, br * nj + j))
                for br in range(4)]
    bg_specs = [pl.BlockSpec((None, 1, tn), lambda i, j, br=br: (layer, 0, br * nj + j))
                for br in range(4)]
    bg = b_gate.reshape(DEPTH, 1, 4 * d)
    return pl.pallas_call(
        _merge_kernel,
        grid=(t // tm, nj),
        in_specs=[o_spec] * 4 + [gc_spec, wb_spec] + wg_specs + bg_specs,
        out_specs=pl.BlockSpec((tm, tn), lambda i, j: (i, j)),
        out_shape=jax.ShapeDtypeStruct((t, d), BF16),
        compiler_params=_params("parallel", "arbitrary"),
        name="merge",
    )(*outs, proj, w_branch, *([w_gate_up] * 4), *([bg] * 4))


def _gate_up_kernel(h_ref, wg_ref, wu_ref, o_ref, *, n_real):
    j = pl.program_id(1)

    @pl.when(j < n_real)
    def _():
        h = h_ref[...]
        gate = _dot(h, wg_ref[...].astype(BF16))
        up = _dot(h, wu_ref[...].astype(BF16))
        o_ref[...] = (_silu(gate) * up).astype(BF16)

    @pl.when(j >= n_real)
    def _():
        o_ref[...] = jnp.zeros_like(o_ref)


def _gate_up(h, w_gate, w_up, layer, tm=2048, tn=FFN_BLOCK):
    t, d = h.shape
    n_real = FFN_HIDDEN // tn
    w_spec = pl.BlockSpec((None, d, tn), lambda i, j: (layer, 0, jnp.minimum(j, n_real - 1)))
    return pl.pallas_call(
        functools.partial(_gate_up_kernel, n_real=n_real),
        grid=(t // tm, FFN_PAD // tn),
        in_specs=[pl.BlockSpec((tm, d), lambda i, j: (i, 0)), w_spec, w_spec],
        out_specs=pl.BlockSpec((tm, tn), lambda i, j: (i, j)),
        out_shape=jax.ShapeDtypeStruct((t, FFN_PAD), BF16),
        compiler_params=_params("parallel", "arbitrary"),
        name="ffn_gate_up",
    )(h, w_gate, w_up)


def _down_kernel(u_ref, w_ref, wt_ref, o_ref, *, n_main):
    k = pl.program_id(2)
    u = u_ref[...]

    @pl.when(k == 0)
    def _():
        o_ref[...] = _dot(u, w_ref[...].astype(BF16))

    @pl.when((k > 0) & (k < n_main))
    def _():
        o_ref[...] += _dot(u, w_ref[...].astype(BF16))

    @pl.when(k == n_main)
    def _():
        o_ref[...] += _dot(u, wt_ref[...].astype(BF16))


def _down(u, w_down, w_down_tail, layer, tm=2048, tn=1024, tk=FFN_KBLOCK):
    t = u.shape[0]
    d = w_down.shape[2]
    n_main = FFN_HIDDEN // tk
    return pl.pallas_call(
        functools.partial(_down_kernel, n_main=n_main),
        grid=(t // tm, d // tn, FFN_PAD // tk),
        in_specs=[pl.BlockSpec((tm, tk), lambda i, j, k: (i, k)),
                  pl.BlockSpec((None, tk, tn), lambda i, j, k: (layer, jnp.minimum(k, n_main - 1), j)),
                  pl.BlockSpec((None, tk, tn), lambda i, j, k: (layer, 0, j))],
        out_specs=pl.BlockSpec((tm, tn), lambda i, j, k: (i, j)),
        out_shape=jax.ShapeDtypeStruct((t, d), F32),
        compiler_params=_params("parallel", "parallel", "arbitrary"),
        name="ffn_down",
    )(u, w_down, w_down_tail)


def _lower_bound_kernel(l_ref, o_ref):
    lg = l_ref[...]
    e = jnp.exp(lg - jnp.max(lg, axis=0, keepdims=True))
    p = e / jnp.sum(e, axis=0, keepdims=True)
    run = p[0:1]
    o_ref[0:1, :] = run - p[0:1]
    for l in range(1, DEPTH):
        run = run + p[l:l + 1]
        o_ref[l:l + 1, :] = run - p[0:1]


def _lower_bounds(logits):
    return pl.pallas_call(
        _lower_bound_kernel,
        out_shape=jax.ShapeDtypeStruct(logits.shape, F32),
        name="hgrn_lower_bounds",
    )(logits)


HGRN_LEVELS = (32, 16, 8, 4, 2, 1)


def _hgrn_exponent_matrix():
    c = HGRN_HALF
    mat = np.zeros(((len(HGRN_LEVELS) + 2) * c, c), np.float32)
    for li, m in enumerate(HGRN_LEVELS):
        for t in range(c):
            r = (t // (2 * m)) * 2 * m + m - 1
            if (t // m) % 2 == 1:
                mat[li * c + t, r + 1:t + 1] = 1.0
            else:
                mat[li * c + t, t + 1:r + 1] = 1.0
    for t in range(c):
        mat[len(HGRN_LEVELS) * c + t, :t + 1] = 1.0
        mat[(len(HGRN_LEVELS) + 1) * c + t, t + 1:] = 1.0
    return np.concatenate([mat, mat, mat], axis=1)


HGRN_SIZES = (HGRN_HALF,) + HGRN_LEVELS
HGRN_COARSE = sum(1 for m in HGRN_SIZES if m % BF16_ROWS == 0)


def _hgrn_init_masks(smask_ref, rmask_ref):
    c = HGRN_CHUNK
    row = lax.broadcasted_iota(jnp.int32, (c, c), 0)
    col = lax.broadcasted_iota(jnp.int32, (c, c), 1)
    for li, m in enumerate(HGRN_SIZES):
        shift = m.bit_length()
        smask_ref[li] = jnp.where((row >> shift) == (col >> shift), 1.0, 0.0)
    smask_ref[len(HGRN_SIZES)] = jnp.where(row == col, 1.0, 0.0)
    rows = lax.broadcasted_iota(jnp.int32, (c, HGRN_DIM), 0)
    for j, m in enumerate(HGRN_SIZES[HGRN_COARSE:]):
        upper = ((rows >> (m.bit_length() - 1)) & 1) == 1
        rmask_ref[0, j] = jnp.where(upper, 1.0, 0.0).astype(BF16)
        rmask_ref[1, j] = jnp.where(upper, 0.0, 1.0).astype(BF16)


def _hgrn_body(q_ref, f_ref, i_ref, g_ref, lb_ref, ng_ref, mat_ref, o_ref, state_ref,
               smask_ref, rmask_ref):
    c = HGRN_CHUNK
    hc = HGRN_HALF
    nl = len(HGRN_LEVELS)

    lb = lb_ref[...]
    forget = lb + (1.0 - lb) * _sigmoid(f_ref[...])
    key = 1.0 - forget
    query = _silu(q_ref[...])
    log_f = jnp.log(forget)
    sums = [_dot(mat_ref[...], _split3(log_f[s * hc:(s + 1) * hc], 0)) for s in range(2)]

    def both(group):
        return jnp.concatenate([sm[group * hc:(group + 1) * hc] for sm in sums], axis=0)

    prefix = [sm[nl * hc:(nl + 1) * hc] for sm in sums]
    suffix = [sm[(nl + 1) * hc:(nl + 2) * hc] for sm in sums]
    total = [p[hc - 1:hc] for p in prefix]
    level_exp = [jnp.concatenate([suffix[0], prefix[1]], axis=0)] + [both(li) for li in range(nl)]
    b_in = jnp.exp(jnp.concatenate([prefix[0], prefix[1] + total[0]], axis=0))
    b_out = jnp.exp(jnp.concatenate([suffix[0] + total[1], suffix[1]], axis=0))
    chunk_decay = jnp.exp(total[0] + total[1])

    for h in range(HGRN_HEADS):
        sl = slice(h * HGRN_DIM, (h + 1) * HGRN_DIM)
        qh, kh = query[:, sl], key[:, sl]
        qb, kb = qh.astype(BF16), kh.astype(BF16)
        vh = i_ref[:, sl].astype(BF16)
        scores = _dot(qb, kb, NT_DIMS) * smask_ref[len(HGRN_SIZES)]
        for li, m in enumerate(HGRN_SIZES):
            dl = jnp.exp(level_exp[li][:, sl]).astype(BF16)
            if m % BF16_ROWS == 0:
                zero = jnp.zeros((m, HGRN_DIM), BF16)
                q_parts, k_parts = [], []
                for blk in range(c // m):
                    rs = slice(blk * m, (blk + 1) * m)
                    is_upper = blk % 2 == 1
                    q_parts.append(qb[rs] * dl[rs] if is_upper else zero)
                    k_parts.append(zero if is_upper else kb[rs] * dl[rs])
                qt = jnp.concatenate(q_parts, axis=0)
                kt = jnp.concatenate(k_parts, axis=0)
            else:
                fine = li - HGRN_COARSE
                qt = qb * dl * rmask_ref[0, fine]
                kt = kb * dl * rmask_ref[1, fine]
            level = _dot(qt, kt, NT_DIMS)
            scores = scores + (level if 2 * m == c else level * smask_ref[li])
        q_in = (qh * b_in[:, sl]).astype(BF16)
        k_out = (kh * b_out[:, sl]).astype(BF16)
        state_t = state_ref[h]
        o = _dot(scores.astype(BF16), vh) + _dot(q_in, state_t.astype(BF16), NT_DIMS)
        state_ref[h] = state_t * chunk_decay[:, sl] + _dot(vh, k_out, TN_DIMS)
        gh = g_ref[:, sl]
        o_ref[:, sl] = (_rms(o, 1e-6) * ng_ref[:, sl] * _silu(gh)).astype(BF16)


def _attn_slope(h):
    return 2.0 ** (-8.0 * (h + 1) / ATTN_Q_HEADS)


def _attn_init_bias(bias_ref):
    c = ATTN_BLOCK
    qi = lax.broadcasted_iota(jnp.int32, (c, 2 * c), 0)
    kj = lax.broadcasted_iota(jnp.int32, (c, 2 * c), 1)
    dist = qi + c - kj
    in_window = (dist >= 0) & (dist < c)
    dist_f = dist.astype(F32)
    for h in range(ATTN_Q_HEADS):
        alibi = -_attn_slope(h) * dist_f
        bias_ref[0, h] = jnp.where(in_window & (kj >= c), alibi, NEG_BIG)
        bias_ref[1, h] = jnp.where(in_window, alibi, NEG_BIG)


def _attn_body(sink_ref, q_ref, k_ref, v_ref, o_ref, prev_ref, bias_ref):
    c = ATTN_BLOCK
    kvw = ATTN_KV_HEADS * ATTN_DIM
    which = jnp.where(pl.program_id(0) == 0, 0, 1)

    k_all =jnp.concatenate([prev_ref[:, :kvw], k_ref[...]], axis=0).astype(BF16)
    v_all = jnp.concatenate([prev_ref[:, kvw:], v_ref[...]], axis=0).astype(BF16)
    group = ATTN_Q_HEADS // ATTN_KV_HEADS
    outs = []
    for h in range(ATTN_Q_HEADS):
        g = h // group
        qh = (q_ref[:, h * ATTN_DIM:(h + 1) * ATTN_DIM] * (ATTN_DIM ** -0.5)).astype(BF16)
        kg = k_all[:, g * ATTN_DIM:(g + 1) * ATTN_DIM]
        vg = v_all[:, g * ATTN_DIM:(g + 1) * ATTN_DIM]
        logits = _dot(qh, kg, NT_DIMS) + bias_ref[which, h]
        sink = sink_ref[h]
        mx = jnp.maximum(jnp.max(logits, axis=-1, keepdims=True), sink)
        p = jnp.exp(logits - mx)
        denom = jnp.sum(p, axis=-1, keepdims=True) + jnp.exp(sink - mx)
        outs.append(_dot(p.astype(BF16), vg) / denom)
    o_ref[...] = jnp.concatenate(outs, axis=1).astype(BF16)
    prev_ref[:, :kvw] = k_ref[...]
    prev_ref[:, kvw:] = v_ref[...]


def _ssd_body(xs_ref, sb_ref, sc_ref, z_ref, dt_ref, cw_ref, cb_ref, dtb_ref, alog_ref,
              dskip_ref, ng_ref, expand_ref, o_ref, prev_ref, state_ref):
    c = SSD_CHUNK
    width = SSD_HEADS * SSD_DIM
    gw = SSD_GROUPS * SSD_STATE
    per_group = SSD_HEADS // SSD_GROUPS

    def conv_silu(x_ref, lo, hi):
        x = x_ref[...]
        ext = jnp.concatenate([prev_ref[:, lo:hi], x], axis=0)
        conv = x * cw_ref[SSD_CONV - 1:SSD_CONV, lo:hi] + cb_ref[:, lo:hi]
        for back in range(1, SSD_CONV):
            shifted = ext[SSD_TAIL - back:SSD_TAIL - back + c]
            conv = conv + shifted * cw_ref[SSD_CONV - 1 - back:SSD_CONV - back, lo:hi]
        prev_ref[:, lo:hi] = x[c - SSD_TAIL:]
        return _silu(conv)

    xs = conv_silu(xs_ref, 0, width)
    bm = conv_silu(sb_ref, width, width + gw).astype(BF16)
    cm = conv_silu(sc_ref, width + gw, width + 2 * gw).astype(BF16)

    dt_in = dt_ref[...] + dtb_ref[...]
    dt = jnp.maximum(dt_in, 0.0) + jnp.log1p(jnp.exp(-jnp.abs(dt_in)))
    b = _cumsum_rows(dt * (-jnp.exp(alog_ref[...])))
    b_t = b.T
    wide = _dot(_split3(jnp.concatenate([dt, b], axis=0), 1), expand_ref[...])
    dt_w, b_w = wide[:c], wide[c:]
    b_last = b_w[c - 1:c, :]
    v = xs * dt_w
    v_bf = v.astype(BF16)
    v_out = (v * jnp.exp(b_last - b_w)).astype(BF16)
    in_scale = jnp.exp(b_w)

    ti = lax.broadcasted_iota(jnp.int32, (c, c), 0)
    si = lax.broadcasted_iota(jnp.int32, (c, c), 1)
    causal = ti >= si
    lane = lax.broadcasted_iota(jnp.int32, (c, 2 * SSD_DIM), 1)
    low = lane < SSD_DIM

    pieces = []
    for g in range(SSD_GROUPS):
        cg = cm[:, g * SSD_STATE:(g + 1) * SSD_STATE]
        bg = bm[:, g * SSD_STATE:(g + 1) * SSD_STATE]
        cb = _dot(cg, bg, NT_DIMS)
        gs = slice(g * per_group * SSD_DIM, (g + 1) * per_group * SSD_DIM)
        state = state_ref[g]
        inter = _dot(cg, state.astype(BF16)) * in_scale[:, gs]
        state_ref[g] = state * jnp.exp(b_last[:, gs]) + _dot(bg, v_out[:, gs], TN_DIMS)
        for pair in range(per_group // 2):
            p_idx = g * (per_group // 2) + pair
            probs = []
            for h in (2 * p_idx, 2 * p_idx + 1):
                w = jnp.exp(jnp.where(causal, b[:, h:h + 1] - b_t[h:h + 1, :], NEG_BIG))
                probs.append((cb * w).astype(BF16))
            v_pair = v_bf[:, p_idx * 2 * SSD_DIM:(p_idx + 1) * 2 * SSD_DIM]
            v_diag = jnp.concatenate([jnp.where(low, v_pair, jnp.zeros_like(v_pair)),
                                      jnp.where(low, jnp.zeros_like(v_pair), v_pair)], axis=0)
            intra = _dot(jnp.concatenate(probs, axis=1), v_diag)
            pieces.append(intra + inter[:, pair * 2 * SSD_DIM:(pair + 1) * 2 * SSD_DIM])
    y = jnp.concatenate(pieces, axis=1) + dskip_ref[...] * xs
    y = y * _silu(z_ref[...])
    o_ref[...] = (_rms(y, 1e-6) * ng_ref[...]).astype(BF16)


def _ssd_expand_matrix():
    width = SSD_HEADS * SSD_DIM
    expand = np.zeros((LANES, width), np.float32)
    for h in range(SSD_HEADS):
        expand[h, h * SSD_DIM:(h + 1) * SSD_DIM] = 1.0
    return np.concatenate([expand] * 3, axis=0)


def _ret_log_gamma(h):
    return math.log(1.0 - 2.0 ** (-5.0 - h))


def _ret_init_mask(mask_ref):
    c = RET_CHUNK
    ti = lax.broadcasted_iota(jnp.int32, (c, c), 0)
    si = lax.broadcasted_iota(jnp.int32, (c, c), 1)
    lag = (ti - si).astype(F32)
    for h in range(RET_HEADS):
        mask_ref[h] = jnp.where(ti >= si, jnp.exp(lag * _ret_log_gamma(h)), 0.0)


def _ret_body(q_ref, k_ref, v_ref, g_ref, o_ref, state_ref, mask_ref):
    c = RET_CHUNK
    pos =lax.broadcasted_iota(jnp.int32, (c, 1), 0).astype(F32)
    for h in range(RET_HEADS):
        lg = _ret_log_gamma(h)
        qh = q_ref[:, h * RET_QK:(h + 1) * RET_QK]
        kh = k_ref[:, h * RET_QK:(h + 1) * RET_QK] * (RET_QK ** -0.5)
        vh = v_ref[:, h * RET_V:(h + 1) * RET_V].astype(BF16)
        scores = _dot(qh.astype(BF16), kh.astype(BF16), NT_DIMS) * mask_ref[h]
        q_in = (qh * jnp.exp((pos + 1.0) * lg)).astype(BF16)
        k_out = (kh * jnp.exp((c - 1.0 - pos) * lg)).astype(BF16)
        state_t = state_ref[h]
        o = _dot(scores.astype(BF16), vh) + _dot(q_in, state_t.astype(BF16), NT_DIMS)
        state_ref[h] = state_t * math.exp(c * lg) + _dot(vh, k_out, TN_DIMS)
        centered = o - jnp.mean(o, axis=-1, keepdims=True)
        normed = centered * lax.rsqrt(jnp.mean(centered * centered, axis=-1, keepdims=True) + 1e-5)
        gh = g_ref[:, h * RET_V:(h + 1) * RET_V]
        o_ref[:, h * RET_V:(h + 1) * RET_V] = (normed * _silu(gh)).astype(BF16)


MIX_ROWS = 128
assert HGRN_CHUNK == ATTN_BLOCK == SSD_CHUNK == RET_CHUNK == MIX_ROWS


def _mixers_kernel(sink_ref,
                   hq_ref, hf_ref, hi_ref, hg_ref, lb_ref, hng_ref, mat_ref,
                   aq_ref, ak_ref, av_ref,
                   xs_ref, sb_ref, sc_ref, sz_ref, dt_ref, cw_ref, cb_ref, dtb_ref, alog_ref,
                   dskip_ref, sng_ref, expand_ref,
                   rq_ref, rk_ref, rv_ref, rg_ref,
                   oh_ref, oa_ref, os_ref, or_ref,
                   hstate_ref, hsmask_ref, hrmask_ref, aprev_ref, abias_ref, sprev_ref, sstate_ref,
                   rstate_ref, rmask_ref):
    @pl.when(pl.program_id(0) == 0)
    def _():
        for ref in (hstate_ref, aprev_ref, sprev_ref, sstate_ref, rstate_ref):
            ref[...] = jnp.zeros_like(ref)
        _hgrn_init_masks(hsmask_ref, hrmask_ref)
        _attn_init_bias(abias_ref)
        _ret_init_mask(rmask_ref)

    _hgrn_body(hq_ref, hf_ref, hi_ref, hg_ref, lb_ref, hng_ref, mat_ref, oh_ref, hstate_ref,
               hsmask_ref, hrmask_ref)
    _attn_body(sink_ref, aq_ref, ak_ref, av_ref, oa_ref, aprev_ref, abias_ref)
    _ssd_body(xs_ref, sb_ref, sc_ref, sz_ref, dt_ref, cw_ref, cb_ref, dtb_ref, alog_ref,
              dskip_ref, sng_ref, expand_ref, os_ref, sprev_ref, sstate_ref)
    _ret_body(rq_ref, rk_ref, rv_ref, rg_ref, or_ref, rstate_ref, rmask_ref)


def _mixers(proj, lower_bound, hgrn_norm_g, sinks, conv_w, conv_b, dt_bias, a_log, d_skip,
            ssd_norm_g):
    t = proj.shape[0]
    c = MIX_ROWS
    bw = BRANCH_WIDTH
    gw = SSD_GROUPS * SSD_STATE
    xw = bw + 2 * gw
    kvw = ATTN_KV_HEADS * ATTN_DIM
    qkw = RET_HEADS * RET_QK
    mat = jnp.asarray(_hgrn_exponent_matrix(), BF16)
    expand = jnp.asarray(_ssd_expand_matrix(), BF16)

    def seg(name, width):
        return pl.BlockSpec((c, width), lambda i, b=COL[name] // width: (i, b))

    def full(a):
        return pl.BlockSpec(a.shape, lambda i: (0, 0))

    def pad_lanes(a):
        return jnp.pad(a.reshape(1, -1), ((0, 0), (0, LANES - a.shape[-1])))

    params = [lower_bound.reshape(1, bw), hgrn_norm_g.reshape(1, bw), mat]
    ssd_params = [conv_w, conv_b.reshape(1, xw), pad_lanes(dt_bias), pad_lanes(a_log),
                  jnp.repeat(d_skip, SSD_DIM).reshape(1, bw), ssd_norm_g.reshape(1, bw), expand]
    in_specs = ([pl.BlockSpec(memory_space=pltpu.SMEM)]
                + [seg(n, bw) for n in ("hq", "hf", "hi", "hg")] + [full(a) for a in params]
                + [seg("aq", bw), seg("ak", kvw), seg("av", kvw)]
                + [seg("xs", bw), seg("sb", gw), seg("sc", gw), seg("sz", bw), seg("dt", LANES)]
                + [full(a) for a in ssd_params]
                + [seg("rq", qkw), seg("rk", qkw), seg("rv", bw), seg("rg", bw)])
    out_spec = pl.BlockSpec((c, bw), lambda i: (i, 0))
    return pl.pallas_call(
        _mixers_kernel,
        grid=(t // c,),
        in_specs=in_specs,
        out_specs=[out_spec] * 4,
        out_shape=[jax.ShapeDtypeStruct((t, bw), BF16)] * 4,
        scratch_shapes=[pltpu.VMEM((HGRN_HEADS, HGRN_DIM, HGRN_DIM), F32),
                        pltpu.VMEM((len(HGRN_SIZES) + 1, c, c), F32),
                        pltpu.VMEM((2, len(HGRN_SIZES) - HGRN_COARSE, c, HGRN_DIM), BF16),
                        pltpu.VMEM((c, 2 * kvw), F32),
                        pltpu.VMEM((2, ATTN_Q_HEADS, c, 2 * c), F32),
                        pltpu.VMEM((SSD_TAIL, xw), F32),
                        pltpu.VMEM((SSD_GROUPS, SSD_STATE, bw // SSD_GROUPS), F32),
                        pltpu.VMEM((RET_HEADS, RET_V, RET_QK), F32),
                        pltpu.VMEM((RET_HEADS, c, c), F32)],
        compiler_params=_params("arbitrary"),
        name="mixers",
    )(sinks, *([proj] * 4), *params, *([proj] * 3), *([proj] * 5), *ssd_params, *([proj] * 4))


def _w_down_tail(w_down):
    start = (FFN_HIDDEN // FFN_KBLOCK) * FFN_KBLOCK
    return jnp.pad(w_down[:, start:, :], ((0, 0), (0, FFN_PAD - FFN_HIDDEN), (0, 0)))


def kernel(x, norm_mix_pre, norm_mix_post, norm_ffn_pre, norm_ffn_post, w_in,
           hgrn_lb_logits, hgrn_norm, attn_sinks, ssd_conv_w, ssd_conv_b, ssd_dt_bias,
           ssd_a_log, ssd_d, ssd_norm, w_gate_up, b_gate, w_branch, w_out,
           w_ffn_gate, w_ffn_up, w_ffn_down):
    b_, t_, d_ = x.shape
    xr = x.reshape(b_ * t_, d_)
    lower_bounds = _lower_bounds(hgrn_lb_logits)
    w_down_tail = _w_down_tail(w_ffn_down)
    h = _norm(xr, norm_mix_pre[0])
    for l in range(DEPTH):
        proj = _in_proj(h, w_in, l)
        outs = _mixers(proj, lower_bounds[l], hgrn_norm[l], attn_sinks[l], ssd_conv_w[l],
                       ssd_conv_b[l], ssd_dt_bias[l], ssd_a_log[l], ssd_d[l], ssd_norm[l])
        merged = _merge(outs, proj, w_branch, w_gate_up, b_gate, l)
        y = _rows_resident_matmul(merged, w_out, l, F32, name="out_proj")
        xr, h = _resid_norm(xr, y, norm_mix_post[l], norm_ffn_pre[l])
        u = _gate_up(h, w_ffn_gate, w_ffn_up, l)
        ff = _down(u, w_ffn_down, w_down_tail, l)
        g_next = norm_mix_pre[l + 1] if l + 1 < DEPTH else None
        xr, h = _resid_norm(xr, ff, norm_ffn_post[l], g_next)
    return xr.reshape(b_, t_, d_)
```

```python
import functools
import math

import numpy as np
import jax
import jax.numpy as jnp
from jax import lax
from jax.experimental import pallas as pl
from jax.experimental.pallas import tpu as pltpu

F32 = jnp.float32
BF16 = jnp.bfloat16

D_MODEL = 4096
DEPTH = 4
BRANCH_WIDTH = D_MODEL // 4
HGRN_HEADS, HGRN_DIM, HGRN_CHUNK = 8, 128, 128
HGRN_HALF = HGRN_CHUNK // 2
ATTN_Q_HEADS, ATTN_KV_HEADS, ATTN_DIM, ATTN_BLOCK = 16, 4, 64, 128
SSD_HEADS, SSD_DIM, SSD_GROUPS, SSD_STATE, SSD_CONV, SSD_CHUNK = 16, 64, 4, 128, 4, 128
SSD_TAIL = 8
RET_HEADS, RET_QK, RET_V, RET_CHUNK = 8, 64, 128, 128
GATE_RANK = 256
FFN_HIDDEN = -(-8 * D_MODEL // (3 * 256)) * 256
FFN_BLOCK = 256
FFN_KBLOCK = 1024
FFN_PAD = -(-FFN_HIDDEN // FFN_KBLOCK) * FFN_KBLOCK
IN_WIDTH = 12048

LANES = 128
BF16_ROWS = 16
VMEM_LIMIT = 56 * 1024 * 1024

IN_BLOCK = 256
_FIRST_ORDER = list(range(20)) + list(range(22, 30)) + [20, 21] + list(range(30, 34))
PROJ_SRC = ([9744 + IN_BLOCK * i for i in range(4)] + [10768 + IN_BLOCK * i for i in range(4)]
            + [IN_BLOCK * j for j in _FIRST_ORDER]
            + [8720, 8720 + IN_BLOCK, 9232, 9232 + IN_BLOCK, 11792, 8704])
PROJ_WIDTH = IN_BLOCK * len(PROJ_SRC)
assert set(range(IN_WIDTH)) == {s + i for s in PROJ_SRC for i in range(IN_BLOCK)}
COL = {"rv": 0, "rg": 1024, "hq": 2048, "hf": 3072, "hi": 4096, "hg": 5120, "aq": 6144,
       "sz": 7168, "xs": 8192, "ak": 9216, "av": 9472, "sb": 9728, "sc": 10240,
       "rq": 10752, "rk": 11264, "gc": 11776, "dt": 12032}


NT_DIMS = (((1,), (1,)), ((), ()))
TN_DIMS = (((0,), (0,)), ((), ()))
NEG_BIG = -1e30
LOG2_E = 1.4426950408889634


def _sigmoid(x):
    return 1.0 / (1.0 + jnp.exp2(x * (-LOG2_E)))


def _silu(x):
    return x * _sigmoid(x)


def _rms(x, eps):
    return x * lax.rsqrt(jnp.mean(x * x, axis=-1, keepdims=True) + eps)


def _dot(a, b, dims=None):
    if dims is None:
        dims = (((a.ndim - 1,), (0,)), ((), ()))
    return lax.dot_general(a, b, dims, preferred_element_type=F32)


def _split3(x, axis):
    hi = x.astype(BF16)
    rest = x - hi.astype(F32)
    mid = rest.astype(BF16)
    lo = (rest - mid.astype(F32)).astype(BF16)
    return jnp.concatenate([hi, mid, lo], axis=axis)


def _cumsum_rows(x):
    n = x.shape[0]
    row = lax.broadcasted_iota(jnp.int32, x.shape, 0)
    k = 1
    while k < n:
        x = x + jnp.where(row >= k, pltpu.roll(x, k, 0), 0.0)
        k *= 2
    return x


def _params(*sem):
    return pltpu.CompilerParams(dimension_semantics=sem, vmem_limit_bytes=VMEM_LIMIT)


def _rows_resident_kernel(a_ref, w0_ref, w1_ref, o_ref):
    a = a_ref[...]
    tn = w0_ref.shape[1]
    o_ref[:, :tn] = _dot(a, w0_ref[...].astype(BF16)).astype(o_ref.dtype)
    o_ref[:, tn:] = _dot(a, w1_ref[...].astype(BF16)).astype(o_ref.dtype)


def _rows_resident_matmul(a, w, layer, out_dtype, tm=2048, tn=IN_BLOCK, name="matmul"):
    m, kd = a.shape
    n = w.shape[2]
    return pl.pallas_call(
        _rows_resident_kernel,
        grid=(m // tm, n // (2 * tn)),
        in_specs=[pl.BlockSpec((tm, kd), lambda i, j: (i, 0), pipeline_mode=pl.Buffered(1))]
        + [pl.BlockSpec((None, kd, tn), lambda i, j, k=k: (layer, 0, 2 * j + k)) for k in range(2)],
        out_specs=pl.BlockSpec((tm, 2 * tn), lambda i, j: (i, j)),
        out_shape=jax.ShapeDtypeStruct((m, n), out_dtype),
        compiler_params=_params("parallel", "arbitrary"),
        name=name,
    )(a, w, w)


def _in_proj_kernel(src_ref, a_ref, wt0_ref, wt1_ref, o_ref):
    del src_ref
    a = a_ref[...]
    o_ref[:, :IN_BLOCK] = _dot(a, wt0_ref[...].astype(BF16), NT_DIMS)
    o_ref[:, IN_BLOCK:] = _dot(a, wt1_ref[...].astype(BF16), NT_DIMS)


def _in_proj(h, w_in, layer, tm=2048):
    t, d = h.shape
    w_t = jnp.swapaxes(w_in, 1, 2)
    src = jnp.asarray(PROJ_SRC, jnp.int32)
    grid_spec = pltpu.PrefetchScalarGridSpec(
        num_scalar_prefetch=1,
        grid=(t // tm, len(PROJ_SRC) // 2),
        in_specs=[pl.BlockSpec((tm, d), lambda i, p, src: (i, 0), pipeline_mode=pl.Buffered(1))]
        + [pl.BlockSpec((None, pl.Element(IN_BLOCK), pl.Element(d)),
                        lambda i, p, src, k=k: (layer, pl.multiple_of(src[2 * p + k], 8), 0))
           for k in range(2)],
        out_specs=pl.BlockSpec((tm, 2 * IN_BLOCK), lambda i, p, src: (i, p)),
    )
    return pl.pallas_call(
        _in_proj_kernel,
        grid_spec=grid_spec,
        out_shape=jax.ShapeDtypeStruct((t, PROJ_WIDTH), F32),
        compiler_params=_params("parallel", "arbitrary"),
        name="in_proj",
    )(src, h, w_t, w_t)


def _norm_kernel(x_ref, g_ref, h_ref):
    h_ref[...] = (_rms(x_ref[...], 1e-6) * g_ref[...]).astype(BF16)


def _norm(x, g, rows=256):
    t, d = x.shape
    return pl.pallas_call(
        _norm_kernel,
        grid=(t // rows,),
        in_specs=[pl.BlockSpec((rows, d), lambda i: (i, 0)),
                  pl.BlockSpec((1, d), lambda i: (0, 0))],
        out_specs=pl.BlockSpec((rows, d), lambda i: (i, 0)),
        out_shape=jax.ShapeDtypeStruct((t, d), BF16),
        compiler_params=_params("parallel"),
        name="norm",
    )(x, g.reshape(1, d))


def _resid_norm_kernel(x_ref, y_ref, gpost_ref, gnext_ref, xo_ref, h_ref):
    xn = x_ref[...] + _rms(y_ref[...], 1e-6) * gpost_ref[...]
    xo_ref[...] = xn
    h_ref[...] = (_rms(xn, 1e-6) * gnext_ref[...]).astype(BF16)


def _resid_kernel(x_ref, y_ref, gpost_ref, xo_ref):
    xo_ref[...] = x_ref[...] + _rms(y_ref[...], 1e-6) * gpost_ref[...]


def _resid_norm(x, y, g_post, g_next, rows=256):
    t, d = x.shape
    row_spec = pl.BlockSpec((rows, d), lambda i: (i, 0))
    vec_spec = pl.BlockSpec((1, d), lambda i: (0, 0))
    if g_next is None:
        return pl.pallas_call(
            _resid_kernel, grid=(t // rows,),
            in_specs=[row_spec, row_spec, vec_spec],
            out_specs=row_spec,
            out_shape=jax.ShapeDtypeStruct((t, d), F32),
            compiler_params=_params("parallel"),
            name="resid",
        )(x, y, g_post.reshape(1, d)), None
    return pl.pallas_call(
        _resid_norm_kernel, grid=(t // rows,),
        in_specs=[row_spec, row_spec, vec_spec, vec_spec],
        out_specs=[row_spec, row_spec],
        out_shape=[jax.ShapeDtypeStruct((t, d), F32), jax.ShapeDtypeStruct((t, d), BF16)],
        compiler_params=_params("parallel"),
        name="resid_norm",
    )(x, y, g_post.reshape(1, d), g_next.reshape(1, d))


def _merge_kernel(oa_ref, ob_ref, oc_ref, od_ref, gc_ref, wb_ref,
                  wg0_ref, wg1_ref, wg2_ref, wg3_ref, bg0_ref, bg1_ref, bg2_ref, bg3_ref,
                  out_ref):
    gc = gc_ref[...].astype(BF16)
    branches = ((oa_ref, wg0_ref, bg0_ref), (ob_ref, wg1_ref, bg1_ref),
                (oc_ref, wg2_ref, bg2_ref), (od_ref, wg3_ref, bg3_ref))
    acc = None
    for br, (o_ref, wg_ref, bg_ref) in enumerate(branches):
        y = _dot(o_ref[...], wb_ref[br].astype(BF16))
        gate = _sigmoid(_dot(gc, wg_ref[...].astype(BF16)) + bg_ref[...])
        acc = gate * y if acc is None else acc + gate * y
    out_ref[...] = acc.astype(BF16)


def _merge(outs, proj, w_branch, w_gate_up, b_gate, layer, tm=2048, tn=256):
    t = proj.shape[0]
    d = D_MODEL
    nj = d // tn
    once = pl.Buffered(1)
    o_spec = pl.BlockSpec((tm, BRANCH_WIDTH), lambda i, j: (i, 0))
    gc_spec = pl.BlockSpec((tm, GATE_RANK), lambda i, j: (i, COL["gc"] // GATE_RANK),
                           pipeline_mode=once)
    wb_spec = pl.BlockSpec((None, 4, BRANCH_WIDTH, tn), lambda i, j: (layer, 0, 0, j))
    wg_specs = [pl.BlockSpec((None, GATE_RANK, tn), lambda i, j, br=br: (layer, 0, br * nj + j))
                for br in range(4)]
    bg_specs = [pl.BlockSpec((None, 1, tn), lambda i, j, br=br: (layer, 0, br * nj + j))
                for br in range(4)]
    bg = b_gate.reshape(DEPTH, 1, 4 * d)
    return pl.pallas_call(
        _merge_kernel,
        grid=(t // tm, nj),
        in_specs=[o_spec] * 4 + [gc_spec, wb_spec] + wg_specs + bg_specs,
        out_specs=pl.BlockSpec((tm, tn), lambda i, j: (i, j)),
        out_shape=jax.ShapeDtypeStruct((t, d), BF16),
        compiler_params=_params("parallel", "arbitrary"),
        name="merge",
    )(*outs, proj, w_branch, *([w_gate_up] * 4), *([bg] * 4))


def _gate_up_kernel(h_ref, wg_ref, wu_ref, o_ref, *, n_real):
    j = pl.program_id(1)

    @pl.when(j < n_real)
    def _():
        h = h_ref[...]
        gate = _dot(h, wg_ref[...].astype(BF16))
        up = _dot(h, wu_ref[...].astype(BF16))
        o_ref[...] = (_silu(gate) * up).astype(BF16)

    @pl.when(j >= n_real)
    def _():
        o_ref[...] = jnp.zeros_like(o_ref)


def _gate_up(h, w_gate, w_up, layer, tm=2048, tn=FFN_BLOCK):
    t, d = h.shape
    n_real = FFN_HIDDEN // tn
    w_spec = pl.BlockSpec((None, d, tn), lambda i, j: (layer, 0, jnp.minimum(j, n_real - 1)))
    return pl.pallas_call(
        functools.partial(_gate_up_kernel, n_real=n_real),
        grid=(t // tm, FFN_PAD // tn),
        in_specs=[pl.BlockSpec((tm, d), lambda i, j: (i, 0)), w_spec, w_spec],
        out_specs=pl.BlockSpec((tm, tn), lambda i, j: (i, j)),
        out_shape=jax.ShapeDtypeStruct((t, FFN_PAD), BF16),
        compiler_params=_params("parallel", "arbitrary"),
        name="ffn_gate_up",
    )(h, w_gate, w_up)


def _down_kernel(u_ref, w_ref, wt_ref, o_ref, *, n_main):
    k = pl.program_id(2)
    u = u_ref[...]

    @pl.when(k == 0)
    def _():
        o_ref[...] = _dot(u, w_ref[...].astype(BF16))

    @pl.when((k > 0) & (k < n_main))
    def _():
        o_ref[...] += _dot(u, w_ref[...].astype(BF16))

    @pl.when(k == n_main)
    def _():
        o_ref[...] += _dot(u, wt_ref[...].astype(BF16))


def _down(u, w_down, w_down_tail, layer, tm=2048, tn=1024, tk=FFN_KBLOCK):
    t = u.shape[0]
    d = w_down.shape[2]
    n_main = FFN_HIDDEN // tk
    return pl.pallas_call(
        functools.partial(_down_kernel, n_main=n_main),
        grid=(t // tm, d // tn, FFN_PAD // tk),
        in_specs=[pl.BlockSpec((tm, tk), lambda i, j, k: (i, k)),
                  pl.BlockSpec((None, tk, tn), lambda i, j, k: (layer, jnp.minimum(k, n_main - 1), j)),
                  pl.BlockSpec((None, tk, tn), lambda i, j, k: (layer, 0, j))],
        out_specs=pl.BlockSpec((tm, tn), lambda i, j, k: (i, j)),
        out_shape=jax.ShapeDtypeStruct((t, d), F32),
        compiler_params=_params("parallel", "parallel", "arbitrary"),
        name="ffn_down",
    )(u, w_down, w_down_tail)


def _lower_bound_kernel(l_ref, o_ref):
    lg = l_ref[...]
    e = jnp.exp(lg - jnp.max(lg, axis=0, keepdims=True))
    p = e / jnp.sum(e, axis=0, keepdims=True)
    run = p[0:1]
    o_ref[0:1, :] = run - p[0:1]
    for l in range(1, DEPTH):
        run = run + p[l:l + 1]
        o_ref[l:l + 1, :] = run - p[0:1]


def _lower_bounds(logits):
    return pl.pallas_call(
        _lower_bound_kernel,
        out_shape=jax.ShapeDtypeStruct(logits.shape, F32),
        name="hgrn_lower_bounds",
    )(logits)


HGRN_LEVELS = (32, 16, 8, 4, 2, 1)


def _hgrn_exponent_matrix():
    c = HGRN_HALF
    mat = np.zeros(((len(HGRN_LEVELS) + 2) * c, c), np.float32)
    for li, m in enumerate(HGRN_LEVELS):
        for t in range(c):
            r = (t // (2 * m)) * 2 * m + m - 1
            if (t // m) % 2 == 1:
                mat[li * c + t, r + 1:t + 1] = 1.0
            else:
                mat[li * c + t, t + 1:r + 1] = 1.0
    for t in range(c):
        mat[len(HGRN_LEVELS) * c + t, :t + 1] = 1.0
        mat[(len(HGRN_LEVELS) + 1) * c + t, t + 1:] = 1.0
    return np.concatenate([mat, mat, mat], axis=1)


HGRN_SIZES = (HGRN_HALF,) + HGRN_LEVELS
HGRN_COARSE = sum(1 for m in HGRN_SIZES if m % BF16_ROWS == 0)


def _hgrn_init_masks(smask_ref, rmask_ref):
    c = HGRN_CHUNK
    row = lax.broadcasted_iota(jnp.int32, (c, c), 0)
    col = lax.broadcasted_iota(jnp.int32, (c, c), 1)
    for li, m in enumerate(HGRN_SIZES):
        shift = m.bit_length()
        smask_ref[li] = jnp.where((row >> shift) == (col >> shift), 1.0, 0.0)
    smask_ref[len(HGRN_SIZES)] = jnp.where(row == col, 1.0, 0.0)
    rows = lax.broadcasted_iota(jnp.int32, (c, HGRN_DIM), 0)
    for j, m in enumerate(HGRN_SIZES[HGRN_COARSE:]):
        upper = ((rows >> (m.bit_length() - 1)) & 1) == 1
        rmask_ref[0, j] = jnp.where(upper, 1.0, 0.0).astype(BF16)
        rmask_ref[1, j] = jnp.where(upper, 0.0, 1.0).astype(BF16)


def _hgrn_body(q_ref, f_ref, i_ref, g_ref, lb_ref, ng_ref, mat_ref, o_ref, state_ref,
               smask_ref, rmask_ref):
    c = HGRN_CHUNK
    hc = HGRN_HALF
    nl = len(HGRN_LEVELS)

    lb = lb_ref[...]
    forget = lb + (1.0 - lb) * _sigmoid(f_ref[...])
    key = 1.0 - forget
    query = _silu(q_ref[...])
    log_f = jnp.log(forget)
    sums = [_dot(mat_ref[...], _split3(log_f[s * hc:(s + 1) * hc], 0)) for s in range(2)]

    def both(group):
        return jnp.concatenate([sm[group * hc:(group + 1) * hc] for sm in sums], axis=0)

    prefix = [sm[nl * hc:(nl + 1) * hc] for sm in sums]
    suffix = [sm[(nl + 1) * hc:(nl + 2) * hc] for sm in sums]
    total = [p[hc - 1:hc] for p in prefix]
    level_exp = [jnp.concatenate([suffix[0], prefix[1]], axis=0)] + [both(li) for li in range(nl)]
    b_in = jnp.exp(jnp.concatenate([prefix[0], prefix[1] + total[0]], axis=0))
    b_out = jnp.exp(jnp.concatenate([suffix[0] + total[1], suffix[1]], axis=0))
    chunk_decay = jnp.exp(total[0] + total[1])

    for h in range(HGRN_HEADS):
        sl = slice(h * HGRN_DIM, (h + 1) * HGRN_DIM)
        qh, kh = query[:, sl], key[:, sl]
        qb, kb = qh.astype(BF16), kh.astype(BF16)
        vh = i_ref[:, sl].astype(BF16)
        scores = _dot(qb, kb, NT_DIMS) * smask_ref[len(HGRN_SIZES)]
        for li, m in enumerate(HGRN_SIZES):
            dl = jnp.exp(level_exp[li][:, sl]).astype(BF16)
            if m % BF16_ROWS == 0:
                zero = jnp.zeros((m, HGRN_DIM), BF16)
                q_parts, k_parts = [], []
                for blk in range(c // m):
                    rs = slice(blk * m, (blk + 1) * m)
                    is_upper = blk % 2 == 1
                    q_parts.append(qb[rs] * dl[rs] if is_upper else zero)
                    k_parts.append(zero if is_upper else kb[rs] * dl[rs])
                qt = jnp.concatenate(q_parts, axis=0)
                kt = jnp.concatenate(k_parts, axis=0)
            else:
                fine = li - HGRN_COARSE
                qt = qb * dl * rmask_ref[0, fine]
                kt = kb * dl * rmask_ref[1, fine]
            level = _dot(qt, kt, NT_DIMS)
            scores = scores + (level if 2 * m == c else level * smask_ref[li])
        q_in = (qh * b_in[:, sl]).astype(BF16)
        k_out = (kh * b_out[:, sl]).astype(BF16)
        state_t = state_ref[h]
        o = _dot(scores.astype(BF16), vh) + _dot(q_in, state_t.astype(BF16), NT_DIMS)
        state_ref[h] = state_t * chunk_decay[:, sl] + _dot(vh, k_out, TN_DIMS)
        gh = g_ref[:, sl]
        o_ref[:, sl] = (_rms(o, 1e-6) * ng_ref[:, sl] * _silu(gh)).astype(BF16)


def _attn_slope(h):
    return 2.0 ** (-8.0 * (h + 1) / ATTN_Q_HEADS)


def _attn_init_bias(bias_ref):
    c = ATTN_BLOCK
    qi = lax.broadcasted_iota(jnp.int32, (c, 2 * c), 0)
    kj = lax.broadcasted_iota(jnp.int32, (c, 2 * c), 1)
    dist = qi + c - kj
    in_window = (dist >= 0) & (dist < c)
    dist_f = dist.astype(F32)
    for h in range(ATTN_Q_HEADS):
        alibi = -_attn_slope(h) * dist_f
        bias_ref[0, h] = jnp.where(in_window & (kj >= c), alibi, NEG_BIG)
        bias_ref[1, h] = jnp.where(in_window, alibi, NEG_BIG)


def _attn_body(sink_ref, q_ref, k_ref, v_ref, o_ref, prev_ref, bias_ref):
    c = ATTN_BLOCK
    kvw = ATTN_KV_HEADS * ATTN_DIM
    which = jnp.where(pl.program_id(0) == 0, 0, 1)

    k_all =jnp.concatenate([prev_ref[:, :kvw], k_ref[...]], axis=0).astype(BF16)
    v_all = jnp.concatenate([prev_ref[:, kvw:], v_ref[...]], axis=0).astype(BF16)
    group = ATTN_Q_HEADS // ATTN_KV_HEADS
    outs = []
    for h in range(ATTN_Q_HEADS):
        g = h // group
        qh = (q_ref[:, h * ATTN_DIM:(h + 1) * ATTN_DIM] * (ATTN_DIM ** -0.5)).astype(BF16)
        kg = k_all[:, g * ATTN_DIM:(g + 1) * ATTN_DIM]
        vg = v_all[:, g * ATTN_DIM:(g + 1) * ATTN_DIM]
        logits = _dot(qh, kg, NT_DIMS) + bias_ref[which, h]
        sink = sink_ref[h]
        mx = jnp.maximum(jnp.max(logits, axis=-1, keepdims=True), sink)
        p = jnp.exp(logits - mx)
        denom = jnp.sum(p, axis=-1, keepdims=True) + jnp.exp(sink - mx)
        outs.append(_dot(p.astype(BF16), vg) / denom)
    o_ref[...] = jnp.concatenate(outs, axis=1).astype(BF16)
    prev_ref[:, :kvw] = k_ref[...]
    prev_ref[:, kvw:] = v_ref[...]


def _ssd_body(xs_ref, sb_ref, sc_ref, z_ref, dt_ref, cw_ref, cb_ref, dtb_ref, alog_ref,
              dskip_ref, ng_ref, expand_ref, o_ref, prev_ref, state_ref):
    c = SSD_CHUNK
    width = SSD_HEADS * SSD_DIM
    gw = SSD_GROUPS * SSD_STATE
    per_group = SSD_HEADS // SSD_GROUPS

    def conv_silu(x_ref, lo, hi):
        x = x_ref[...]
        ext = jnp.concatenate([prev_ref[:, lo:hi], x], axis=0)
        conv = x * cw_ref[SSD_CONV - 1:SSD_CONV, lo:hi] + cb_ref[:, lo:hi]
        for back in range(1, SSD_CONV):
            shifted = ext[SSD_TAIL - back:SSD_TAIL - back + c]
            conv = conv + shifted * cw_ref[SSD_CONV - 1 - back:SSD_CONV - back, lo:hi]
        prev_ref[:, lo:hi] = x[c - SSD_TAIL:]
        return _silu(conv)

    xs = conv_silu(xs_ref, 0, width)
    bm = conv_silu(sb_ref, width, width + gw).astype(BF16)
    cm = conv_silu(sc_ref, width + gw, width + 2 * gw).astype(BF16)

    dt_in = dt_ref[...] + dtb_ref[...]
    dt = jnp.maximum(dt_in, 0.0) + jnp.log1p(jnp.exp(-jnp.abs(dt_in)))
    b = _cumsum_rows(dt * (-jnp.exp(alog_ref[...])))
    b_t = b.T
    wide = _dot(_split3(jnp.concatenate([dt, b], axis=0), 1), expand_ref[...])
    dt_w, b_w = wide[:c], wide[c:]
    b_last = b_w[c - 1:c, :]
    v = xs * dt_w
    v_bf = v.astype(BF16)
    v_out = (v * jnp.exp(b_last - b_w)).astype(BF16)
    in_scale = jnp.exp(b_w)

    ti = lax.broadcasted_iota(jnp.int32, (c, c), 0)
    si = lax.broadcasted_iota(jnp.int32, (c, c), 1)
    causal = ti >= si
    lane = lax.broadcasted_iota(jnp.int32, (c, 2 * SSD_DIM), 1)
    low = lane < SSD_DIM

    pieces = []
    for g in range(SSD_GROUPS):
        cg = cm[:, g * SSD_STATE:(g + 1) * SSD_STATE]
        bg = bm[:, g * SSD_STATE:(g + 1) * SSD_STATE]
        cb = _dot(cg, bg, NT_DIMS)
        gs = slice(g * per_group * SSD_DIM, (g + 1) * per_group * SSD_DIM)
        state = state_ref[g]
        inter = _dot(cg, state.astype(BF16)) * in_scale[:, gs]
        state_ref[g] = state * jnp.exp(b_last[:, gs]) + _dot(bg, v_out[:, gs], TN_DIMS)
        for pair in range(per_group // 2):
            p_idx = g * (per_group // 2) + pair
            probs = []
            for h in (2 * p_idx, 2 * p_idx + 1):
                w = jnp.exp(jnp.where(causal, b[:, h:h + 1] - b_t[h:h + 1, :], NEG_BIG))
                probs.append((cb * w).astype(BF16))
            v_pair = v_bf[:, p_idx * 2 * SSD_DIM:(p_idx + 1) * 2 * SSD_DIM]
            v_diag = jnp.concatenate([jnp.where(low, v_pair, jnp.zeros_like(v_pair)),
                                      jnp.where(low, jnp.zeros_like(v_pair), v_pair)], axis=0)
            intra = _dot(jnp.concatenate(probs, axis=1), v_diag)
            pieces.append(intra + inter[:, pair * 2 * SSD_DIM:(pair + 1) * 2 * SSD_DIM])
    y = jnp.concatenate(pieces, axis=1) + dskip_ref[...] * xs
    y = y * _silu(z_ref[...])
    o_ref[...] = (_rms(y, 1e-6) * ng_ref[...]).astype(BF16)


def _ssd_expand_matrix():
    width = SSD_HEADS * SSD_DIM
    expand = np.zeros((LANES, width), np.float32)
    for h in range(SSD_HEADS):
        expand[h, h * SSD_DIM:(h + 1) * SSD_DIM] = 1.0
    return np.concatenate([expand] * 3, axis=0)


def _ret_log_gamma(h):
    return math.log(1.0 - 2.0 ** (-5.0 - h))


def _ret_init_mask(mask_ref):
    c = RET_CHUNK
    ti = lax.broadcasted_iota(jnp.int32, (c, c), 0)
    si = lax.broadcasted_iota(jnp.int32, (c, c), 1)
    lag = (ti - si).astype(F32)
    for h in range(RET_HEADS):
        mask_ref[h] = jnp.where(ti >= si, jnp.exp(lag * _ret_log_gamma(h)), 0.0)


def _ret_body(q_ref, k_ref, v_ref, g_ref, o_ref, state_ref, mask_ref):
    c = RET_CHUNK
    pos =lax.broadcasted_iota(jnp.int32, (c, 1), 0).astype(F32)
    for h in range(RET_HEADS):
        lg = _ret_log_gamma(h)
        qh = q_ref[:, h * RET_QK:(h + 1) * RET_QK]
        kh = k_ref[:, h * RET_QK:(h + 1) * RET_QK] * (RET_QK ** -0.5)
        vh = v_ref[:, h * RET_V:(h + 1) * RET_V].astype(BF16)
        scores = _dot(qh.astype(BF16), kh.astype(BF16), NT_DIMS) * mask_ref[h]
        q_in = (qh * jnp.exp((pos + 1.0) * lg)).astype(BF16)
        k_out = (kh * jnp.exp((c - 1.0 - pos) * lg)).astype(BF16)
        state_t = state_ref[h]
        o = _dot(scores.astype(BF16), vh) + _dot(q_in, state_t.astype(BF16), NT_DIMS)
        state_ref[h] = state_t * math.exp(c * lg) + _dot(vh, k_out, TN_DIMS)
        centered = o - jnp.mean(o, axis=-1, keepdims=True)
        normed = centered * lax.rsqrt(jnp.mean(centered * centered, axis=-1, keepdims=True) + 1e-5)
        gh = g_ref[:, h * RET_V:(h + 1) * RET_V]
        o_ref[:, h * RET_V:(h + 1) * RET_V] = (normed * _silu(gh)).astype(BF16)


MIX_ROWS = 128
assert HGRN_CHUNK == ATTN_BLOCK == SSD_CHUNK == RET_CHUNK == MIX_ROWS


def _mixers_kernel(sink_ref,
                   hq_ref, hf_ref, hi_ref, hg_ref, lb_ref, hng_ref, mat_ref,
                   aq_ref, ak_ref, av_ref,
                   xs_ref, sb_ref, sc_ref, sz_ref, dt_ref, cw_ref, cb_ref, dtb_ref, alog_ref,
                   dskip_ref, sng_ref, expand_ref,
                   rq_ref, rk_ref, rv_ref, rg_ref,
                   oh_ref, oa_ref, os_ref, or_ref,
                   hstate_ref, hsmask_ref, hrmask_ref, aprev_ref, abias_ref, sprev_ref, sstate_ref,
                   rstate_ref, rmask_ref):
    @pl.when(pl.program_id(0) == 0)
    def _():
        for ref in (hstate_ref, aprev_ref, sprev_ref, sstate_ref, rstate_ref):
            ref[...] = jnp.zeros_like(ref)
        _hgrn_init_masks(hsmask_ref, hrmask_ref)
        _attn_init_bias(abias_ref)
        _ret_init_mask(rmask_ref)

    _hgrn_body(hq_ref, hf_ref, hi_ref, hg_ref, lb_ref, hng_ref, mat_ref, oh_ref, hstate_ref,
               hsmask_ref, hrmask_ref)
    _attn_body(sink_ref, aq_ref, ak_ref, av_ref, oa_ref, aprev_ref, abias_ref)
    _ssd_body(xs_ref, sb_ref, sc_ref, sz_ref, dt_ref, cw_ref, cb_ref, dtb_ref, alog_ref,
              dskip_ref, sng_ref, expand_ref, os_ref, sprev_ref, sstate_ref)
    _ret_body(rq_ref, rk_ref, rv_ref, rg_ref, or_ref, rstate_ref, rmask_ref)


def _mixers(proj, lower_bound, hgrn_norm_g, sinks, conv_w, conv_b, dt_bias, a_log, d_skip,
            ssd_norm_g):
    t = proj.shape[0]
    c = MIX_ROWS
    bw = BRANCH_WIDTH
    gw = SSD_GROUPS * SSD_STATE
    xw = bw + 2 * gw
    kvw = ATTN_KV_HEADS * ATTN_DIM
    qkw = RET_HEADS * RET_QK
    mat = jnp.asarray(_hgrn_exponent_matrix(), BF16)
    expand = jnp.asarray(_ssd_expand_matrix(), BF16)

    def seg(name, width):
        return pl.BlockSpec((c, width), lambda i, b=COL[name] // width: (i, b))

    def full(a):
        return pl.BlockSpec(a.shape, lambda i: (0, 0))

    def pad_lanes(a):
        return jnp.pad(a.reshape(1, -1), ((0, 0), (0, LANES - a.shape[-1])))

    params = [lower_bound.reshape(1, bw), hgrn_norm_g.reshape(1, bw), mat]
    ssd_params = [conv_w, conv_b.reshape(1, xw), pad_lanes(dt_bias), pad_lanes(a_log),
                  jnp.repeat(d_skip, SSD_DIM).reshape(1, bw), ssd_norm_g.reshape(1, bw), expand]
    in_specs = ([pl.BlockSpec(memory_space=pltpu.SMEM)]
                + [seg(n, bw) for n in ("hq", "hf", "hi", "hg")] + [full(a) for a in params]
                + [seg("aq", bw), seg("ak", kvw), seg("av", kvw)]
                + [seg("xs", bw), seg("sb", gw), seg("sc", gw), seg("sz", bw), seg("dt", LANES)]
                + [full(a) for a in ssd_params]
                + [seg("rq", qkw), seg("rk", qkw), seg("rv", bw), seg("rg", bw)])
    out_spec = pl.BlockSpec((c, bw), lambda i: (i, 0))
    return pl.pallas_call(
        _mixers_kernel,
        grid=(t // c,),
        in_specs=in_specs,
        out_specs=[out_spec] * 4,
        out_shape=[jax.ShapeDtypeStruct((t, bw), BF16)] * 4,
        scratch_shapes=[pltpu.VMEM((HGRN_HEADS, HGRN_DIM, HGRN_DIM), F32),
                        pltpu.VMEM((len(HGRN_SIZES) + 1, c, c), F32),
                        pltpu.VMEM((2, len(HGRN_SIZES) - HGRN_COARSE, c, HGRN_DIM), BF16),
                        pltpu.VMEM((c, 2 * kvw), F32),
                        pltpu.VMEM((2, ATTN_Q_HEADS, c, 2 * c), F32),
                        pltpu.VMEM((SSD_TAIL, xw), F32),
                        pltpu.VMEM((SSD_GROUPS, SSD_STATE, bw // SSD_GROUPS), F32),
                        pltpu.VMEM((RET_HEADS, RET_V, RET_QK), F32),
                        pltpu.VMEM((RET_HEADS, c, c), F32)],
        compiler_params=_params("arbitrary"),
        name="mixers",
    )(sinks, *([proj] * 4), *params, *([proj] * 3), *([proj] * 5), *ssd_params, *([proj] * 4))


def _w_down_tail(w_down):
    start = (FFN_HIDDEN // FFN_KBLOCK) * FFN_KBLOCK
    return jnp.pad(w_down[:, start:, :], ((0, 0), (0, FFN_PAD - FFN_HIDDEN), (0, 0)))


def kernel(x, norm_mix_pre, norm_mix_post, norm_ffn_pre, norm_ffn_post, w_in,
           hgrn_lb_logits, hgrn_norm, attn_sinks, ssd_conv_w, ssd_conv_b, ssd_dt_bias,
           ssd_a_log, ssd_d, ssd_norm, w_gate_up, b_gate, w_branch, w_out,
           w_ffn_gate, w_ffn_up, w_ffn_down):
    b_, t_, d_ = x.shape
    xr = x.reshape(b_ * t_, d_)
    lower_bounds = _lower_bounds(hgrn_lb_logits)
    w_down_tail = _w_down_tail(w_ffn_down)
    h = _norm(xr, norm_mix_pre[0])
    for l in range(DEPTH):
        proj = _in_proj(h, w_in, l)
        outs = _mixers(proj, lower_bounds[l], hgrn_norm[l], attn_sinks[l], ssd_conv_w[l],
                       ssd_conv_b[l], ssd_dt_bias[l], ssd_a_log[l], ssd_d[l], ssd_norm[l])
        merged = _merge(outs, proj, w_branch, w_gate_up, b_gate, l)
        y = _rows_resident_matmul(merged, w_out, l, F32, name="out_proj")
        xr, h = _resid_norm(xr, y, norm_mix_post[l], norm_ffn_pre[l])
        u = _gate_up(h, w_ffn_gate, w_ffn_up, l)
        ff = _down(u, w_ffn_down, w_down_tail, l)
        g_next = norm_mix_pre[l + 1] if l + 1 < DEPTH else None
        xr, h = _resid_norm(xr, ff, norm_ffn_post[l], g_next)
    return xr.reshape(b_, t_, d_)
```

```python
import functools
import math

import numpy as np
import jax
import jax.numpy as jnp
from jax import lax
from jax.experimental import pallas as pl
from jax.experimental.pallas import tpu as pltpu

F32 = jnp.float32
BF16 = jnp.bfloat16

D_MODEL = 4096
DEPTH = 4
BRANCH_WIDTH = D_MODEL // 4
HGRN_HEADS, HGRN_DIM, HGRN_CHUNK = 8, 128, 128
HGRN_HALF = HGRN_CHUNK // 2
ATTN_Q_HEADS, ATTN_KV_HEADS, ATTN_DIM, ATTN_BLOCK = 16, 4, 64, 128
SSD_HEADS, SSD_DIM, SSD_GROUPS, SSD_STATE, SSD_CONV, SSD_CHUNK = 16, 64, 4, 128, 4, 128
SSD_TAIL = 8
RET_HEADS, RET_QK, RET_V, RET_CHUNK = 8, 64, 128, 128
GATE_RANK = 256
FFN_HIDDEN = -(-8 * D_MODEL // (3 * 256)) * 256
FFN_BLOCK = 256
FFN_KBLOCK = 1024
IN_WIDTH = 12048

LANES = 128
BF16_ROWS = 16
VMEM_LIMIT = 56 * 1024 * 1024

IN_BLOCK = 256
_FIRST_ORDER = list(range(20)) + list(range(22, 30)) + [20, 21] + list(range(30, 34))
PROJ_SRC = ([9744 + IN_BLOCK * i for i in range(4)] + [10768 + IN_BLOCK * i for i in range(4)]
            + [IN_BLOCK * j for j in _FIRST_ORDER]
            + [8720, 8720 + IN_BLOCK, 9232, 9232 + IN_BLOCK, 11792, 8704])
PROJ_WIDTH = IN_BLOCK * len(PROJ_SRC)
assert set(range(IN_WIDTH)) == {s + i for s in PROJ_SRC for i in range(IN_BLOCK)}
COL = {"rv": 0, "rg": 1024, "hq": 2048, "hf": 3072, "hi": 4096, "hg": 5120, "aq": 6144,
       "sz": 7168, "xs": 8192, "ak": 9216, "av": 9472, "sb": 9728, "sc": 10240,
       "rq": 10752, "rk": 11264, "gc": 11776, "dt": 12032}


NT_DIMS = (((1,), (1,)), ((), ()))
TN_DIMS = (((0,), (0,)), ((), ()))
NEG_BIG = -1e30
LOG2_E = 1.4426950408889634


def _sigmoid(x):
    return 1.0 / (1.0 + jnp.exp2(x * (-LOG2_E)))


def _silu(x):
    return x * _sigmoid(x)


def _rms(x, eps):
    return x * lax.rsqrt(jnp.mean(x * x, axis=-1, keepdims=True) + eps)


def _dot(a, b, dims=None):
    if dims is None:
        dims = (((a.ndim - 1,), (0,)), ((), ()))
    return lax.dot_general(a, b, dims, preferred_element_type=F32)


def _split3(x, axis):
    hi = x.astype(BF16)
    rest = x - hi.astype(F32)
    mid = rest.astype(BF16)
    lo = (rest - mid.astype(F32)).astype(BF16)
    return jnp.concatenate([hi, mid, lo], axis=axis)


def _cumsum_rows(x):
    n = x.shape[0]
    row = lax.broadcasted_iota(jnp.int32, x.shape, 0)
    k = 1
    while k < n:
        x = x + jnp.where(row >= k, pltpu.roll(x, k, 0), 0.0)
        k *= 2
    return x


def _params(*sem):
    return pltpu.CompilerParams(dimension_semantics=sem, vmem_limit_bytes=VMEM_LIMIT)


def _rows_resident_kernel(a_ref, w0_ref, w1_ref, o_ref):
    a = a_ref[...]
    tn = w0_ref.shape[1]
    o_ref[:, :tn] = _dot(a, w0_ref[...].astype(BF16)).astype(o_ref.dtype)
    o_ref[:, tn:] = _dot(a, w1_ref[...].astype(BF16)).astype(o_ref.dtype)


def _rows_resident_matmul(a, w, layer, out_dtype, tm=2048, tn=IN_BLOCK, name="matmul"):
    m, kd = a.shape
    n = w.shape[2]
    return pl.pallas_call(
        _rows_resident_kernel,
        grid=(m // tm, n // (2 * tn)),
        in_specs=[pl.BlockSpec((tm, kd), lambda i, j: (i, 0), pipeline_mode=pl.Buffered(1))]
        + [pl.BlockSpec((None, kd, tn), lambda i, j, k=k: (layer, 0, 2 * j + k)) for k in range(2)],
        out_specs=pl.BlockSpec((tm, 2 * tn), lambda i, j: (i, j)),
        out_shape=jax.ShapeDtypeStruct((m, n), out_dtype),
        compiler_params=_params("parallel", "arbitrary"),
        name=name,
    )(a, w, w)


def _in_proj_kernel(src_ref, a_ref, wt0_ref, wt1_ref, o_ref):
    del src_ref
    a = a_ref[...]
    o_ref[:, :IN_BLOCK] = _dot(a, wt0_ref[...].astype(BF16), NT_DIMS)
    o_ref[:, IN_BLOCK:] = _dot(a, wt1_ref[...].astype(BF16), NT_DIMS)


def _in_proj(h, w_in, layer, tm=2048):
    t, d = h.shape
    w_t = jnp.swapaxes(w_in, 1, 2)
    src = jnp.asarray(PROJ_SRC, jnp.int32)
    grid_spec = pltpu.PrefetchScalarGridSpec(
        num_scalar_prefetch=1,
        grid=(t // tm, len(PROJ_SRC) // 2),
        in_specs=[pl.BlockSpec((tm, d), lambda i, p, src: (i, 0), pipeline_mode=pl.Buffered(1))]
        + [pl.BlockSpec((None, pl.Element(IN_BLOCK), pl.Element(d)),
                        lambda i, p, src, k=k: (layer, pl.multiple_of(src[2 * p + k], 8), 0))
           for k in range(2)],
        out_specs=pl.BlockSpec((tm, 2 * IN_BLOCK), lambda i, p, src: (i, p)),
    )
    return pl.pallas_call(
        _in_proj_kernel,
        grid_spec=grid_spec,
        out_shape=jax.ShapeDtypeStruct((t, PROJ_WIDTH), F32),
        compiler_params=_params("parallel", "arbitrary"),
        name="in_proj",
    )(src, h, w_t, w_t)


def _norm_kernel(x_ref, g_ref, h_ref):
    h_ref[...] = (_rms(x_ref[...], 1e-6) * g_ref[...]).astype(BF16)


def _norm(x, g, rows=256):
    t, d = x.shape
    return pl.pallas_call(
        _norm_kernel,
        grid=(t // rows,),
        in_specs=[pl.BlockSpec((rows, d), lambda i: (i, 0)),
                  pl.BlockSpec((1, d), lambda i: (0, 0))],
        out_specs=pl.BlockSpec((rows, d), lambda i: (i, 0)),
        out_shape=jax.ShapeDtypeStruct((t, d), BF16),
        compiler_params=_params("parallel"),
        name="norm",
    )(x, g.reshape(1, d))


def _resid_norm_kernel(x_ref, y_ref, gpost_ref, gnext_ref, xo_ref, h_ref):
    xn = x_ref[...] + _rms(y_ref[...], 1e-6) * gpost_ref[...]
    xo_ref[...] = xn
    h_ref[...] = (_rms(xn, 1e-6) * gnext_ref[...]).astype(BF16)


def _resid_kernel(x_ref, y_ref, gpost_ref, xo_ref):
    xo_ref[...] = x_ref[...] + _rms(y_ref[...], 1e-6) * gpost_ref[...]


def _resid_norm(x, y, g_post, g_next, rows=256):
    t, d = x.shape
    row_spec = pl.BlockSpec((rows, d), lambda i: (i, 0))
    vec_spec = pl.BlockSpec((1, d), lambda i: (0, 0))
    if g_next is None:
        return pl.pallas_call(
            _resid_kernel, grid=(t // rows,),
            in_specs=[row_spec, row_spec, vec_spec],
            out_specs=row_spec,
            out_shape=jax.ShapeDtypeStruct((t, d), F32),
            compiler_params=_params("parallel"),
            name="resid",
        )(x, y, g_post.reshape(1, d)), None
    return pl.pallas_call(
        _resid_norm_kernel, grid=(t // rows,),
        in_specs=[row_spec, row_spec, vec_spec, vec_spec],
        out_specs=[row_spec, row_spec],
        out_shape=[jax.ShapeDtypeStruct((t, d), F32), jax.ShapeDtypeStruct((t, d), BF16)],
        compiler_params=_params("parallel"),
        name="resid_norm",
    )(x, y, g_post.reshape(1, d), g_next.reshape(1, d))


def _merge_kernel(oa_ref, ob_ref, oc_ref, od_ref, gc_ref, wb_ref,
                  wg0_ref, wg1_ref, wg2_ref, wg3_ref, bg0_ref, bg1_ref, bg2_ref, bg3_ref,
                  out_ref):
    gc = gc_ref[...].astype(BF16)
    branches = ((oa_ref, wg0_ref, bg0_ref), (ob_ref, wg1_ref, bg1_ref),
                (oc_ref, wg2_ref, bg2_ref), (od_ref, wg3_ref, bg3_ref))
    acc = None
    for br, (o_ref, wg_ref, bg_ref) in enumerate(branches):
        y = _dot(o_ref[...], wb_ref[br].astype(BF16))
        gate = _sigmoid(_dot(gc, wg_ref[...].astype(BF16)) + bg_ref[...])
        acc = gate * y if acc is None else acc + gate * y
    out_ref[...] = acc.astype(BF16)


def _merge(outs, proj, w_branch, w_gate_up, b_gate, layer, tm=2048, tn=256):
    t = proj.shape[0]
    d = D_MODEL
    nj = d // tn
    once = pl.Buffered(1)
    o_spec = pl.BlockSpec((tm, BRANCH_WIDTH), lambda i, j: (i, 0))
    gc_spec = pl.BlockSpec((tm, GATE_RANK), lambda i, j: (i, COL["gc"] // GATE_RANK),
                           pipeline_mode=once)
    wb_spec = pl.BlockSpec((None, 4, BRANCH_WIDTH, tn), lambda i, j: (layer, 0, 0, j))
    wg_specs = [pl.BlockSpec((None, GATE_RANK, tn), lambda i, j, br=br: (layer, 0, br * nj + j))
                for br in range(4)]
    bg_specs = [pl.BlockSpec((None, 1, tn), lambda i, j, br=br: (layer, 0, br * nj + j))
                for br in range(4)]
    bg = b_gate.reshape(DEPTH, 1, 4 * d)
    return pl.pallas_call(
        _merge_kernel,
        grid=(t // tm, nj),
        in_specs=[o_spec] * 4 + [gc_spec, wb_spec] + wg_specs + bg_specs,
        out_specs=pl.BlockSpec((tm, tn), lambda i, j: (i, j)),
        out_shape=jax.ShapeDtypeStruct((t, d), BF16),
        compiler_params=_params("parallel", "arbitrary"),
        name="merge",
    )(*outs, proj, w_branch, *([w_gate_up] * 4), *([bg] * 4))


def _gate_up_kernel(h_ref, wg_ref, wu_ref, o_ref):
    h = h_ref[...]
    gate = _dot(h, wg_ref[...].astype(BF16))
    up = _dot(h, wu_ref[...].astype(BF16))
    o_ref[...] = (_silu(gate) * up).astype(BF16)


def _gate_up(h, w_gate, w_up, layer, tm=2048, tn=FFN_BLOCK):
    t, d = h.shape
    w_spec = pl.BlockSpec((None, d, tn), lambda i, j: (layer, 0, j))
    return pl.pallas_call(
        _gate_up_kernel,
        grid=(t // tm, FFN_HIDDEN // tn),
        in_specs=[pl.BlockSpec((tm, d), lambda i, j: (i, 0)), w_spec, w_spec],
        out_specs=pl.BlockSpec((tm, tn), lambda i, j: (i, j)),
        out_shape=jax.ShapeDtypeStruct((t, FFN_HIDDEN), BF16),
        compiler_params=_params("parallel", "arbitrary"),
        name="ffn_gate_up",
    )(h, w_gate, w_up)


def _down_kernel(u_ref, w_ref, ut_ref, wt_ref, o_ref, *, n_main):
    k = pl.program_id(2)

    @pl.when(k == 0)
    def _():
        o_ref[...] = _dot(u_ref[...], w_ref[...].astype(BF16))

    @pl.when((k > 0) & (k < n_main))
    def _():
        o_ref[...] += _dot(u_ref[...], w_ref[...].astype(BF16))

    @pl.when(k == n_main)
    def _():
        o_ref[...] += _dot(ut_ref[...], wt_ref[...].astype(BF16))


def _down(u, w_down, layer, tm=2048, tn=1024, tk=FFN_KBLOCK):
    t = u.shape[0]
    d = w_down.shape[2]
    n_main = FFN_HIDDEN // tk
    tail = FFN_HIDDEN - n_main * tk
    return pl.pallas_call(
        functools.partial(_down_kernel, n_main=n_main),
        grid=(t // tm, d // tn, n_main + 1),
        in_specs=[pl.BlockSpec((tm, tk), lambda i, j, k: (i, jnp.minimum(k, n_main - 1))),
                  pl.BlockSpec((None, tk, tn), lambda i, j, k: (layer, jnp.minimum(k, n_main - 1), j)),
                  pl.BlockSpec((pl.Element(tm), pl.Element(tail)),
                               lambda i, j, k: (pl.multiple_of(i * tm, tm), n_main * tk)),
                  pl.BlockSpec((None, pl.Element(tail), pl.Element(tn)),
                               lambda i, j, k: (layer, n_main * tk, pl.multiple_of(j * tn, tn)))],
        out_specs=pl.BlockSpec((tm, tn), lambda i, j, k: (i, j)),
        out_shape=jax.ShapeDtypeStruct((t, d), F32),
        compiler_params=_params("parallel", "parallel", "arbitrary"),
        name="ffn_down",
    )(u, w_down, u, w_down)


def _lower_bound_kernel(l_ref, o_ref):
    lg = l_ref[...]
    e = jnp.exp(lg - jnp.max(lg, axis=0, keepdims=True))
    p = e / jnp.sum(e, axis=0, keepdims=True)
    run = p[0:1]
    o_ref[0:1, :] = run - p[0:1]
    for l in range(1, DEPTH):
        run = run + p[l:l + 1]
        o_ref[l:l + 1, :] = run - p[0:1]


def _lower_bounds(logits):
    return pl.pallas_call(
        _lower_bound_kernel,
        out_shape=jax.ShapeDtypeStruct(logits.shape, F32),
        name="hgrn_lower_bounds",
    )(logits)


HGRN_LEVELS = (32, 16, 8, 4, 2, 1)


def _hgrn_exponent_matrix():
    c = HGRN_HALF
    mat = np.zeros(((len(HGRN_LEVELS) + 2) * c, c), np.float32)
    for li, m in enumerate(HGRN_LEVELS):
        for t in range(c):
            r = (t // (2 * m)) * 2 * m + m - 1
            if (t // m) % 2 == 1:
                mat[li * c + t, r + 1:t + 1] = 1.0
            else:
                mat[li * c + t, t + 1:r + 1] = 1.0
    for t in range(c):
        mat[len(HGRN_LEVELS) * c + t, :t + 1] = 1.0
        mat[(len(HGRN_LEVELS) + 1) * c + t, t + 1:] = 1.0
    return np.concatenate([mat, mat, mat], axis=1)


HGRN_SIZES = (HGRN_HALF,) + HGRN_LEVELS
HGRN_COARSE = sum(1 for m in HGRN_SIZES if m % BF16_ROWS == 0)


def _hgrn_init_masks(smask_ref, rmask_ref):
    c = HGRN_CHUNK
    row = lax.broadcasted_iota(jnp.int32, (c, c), 0)
    col = lax.broadcasted_iota(jnp.int32, (c, c), 1)
    for li, m in enumerate(HGRN_SIZES):
        shift = m.bit_length()
        smask_ref[li] = jnp.where((row >> shift) == (col >> shift), 1.0, 0.0)
    smask_ref[len(HGRN_SIZES)] = jnp.where(row == col, 1.0, 0.0)
    rows = lax.broadcasted_iota(jnp.int32, (c, HGRN_DIM), 0)
    for j, m in enumerate(HGRN_SIZES[HGRN_COARSE:]):
        upper = ((rows >> (m.bit_length() - 1)) & 1) == 1
        rmask_ref[0, j] = jnp.where(upper, 1.0, 0.0).astype(BF16)
        rmask_ref[1, j] = jnp.where(upper, 0.0, 1.0).astype(BF16)


def _hgrn_body(q_ref, f_ref, i_ref, g_ref, lb_ref, ng_ref, mat_ref, o_ref, state_ref,
               smask_ref, rmask_ref):
    c = HGRN_CHUNK
    hc = HGRN_HALF
    nl = len(HGRN_LEVELS)

    lb = lb_ref[...]
    forget = lb + (1.0 - lb) * _sigmoid(f_ref[...])
    key = 1.0 - forget
    query = _silu(q_ref[...])
    log_f = jnp.log(forget)
    sums = [_dot(mat_ref[...], _split3(log_f[s * hc:(s + 1) * hc], 0)) for s in range(2)]

    def both(group):
        return jnp.concatenate([sm[group * hc:(group + 1) * hc] for sm in sums], axis=0)

    prefix = [sm[nl * hc:(nl + 1) * hc] for sm in sums]
    suffix = [sm[(nl + 1) * hc:(nl + 2) * hc] for sm in sums]
    total = [p[hc - 1:hc] for p in prefix]
    level_exp = [jnp.concatenate([suffix[0], prefix[1]], axis=0)] + [both(li) for li in range(nl)]
    b_in = jnp.exp(jnp.concatenate([prefix[0], prefix[1] + total[0]], axis=0))
    b_out = jnp.exp(jnp.concatenate([suffix[0] + total[1], suffix[1]], axis=0))
    chunk_decay = jnp.exp(total[0] + total[1])

    for h in range(HGRN_HEADS):
        sl = slice(h * HGRN_DIM, (h + 1) * HGRN_DIM)
        qh, kh = query[:, sl], key[:, sl]
        qb, kb = qh.astype(BF16), kh.astype(BF16)
        vh = i_ref[:, sl].astype(BF16)
        scores = _dot(qb, kb, NT_DIMS) * smask_ref[len(HGRN_SIZES)]
        for li, m in enumerate(HGRN_SIZES):
            dl = jnp.exp(level_exp[li][:, sl]).astype(BF16)
            if m % BF16_ROWS == 0:
                zero = jnp.zeros((m, HGRN_DIM), BF16)
                q_parts, k_parts = [], []
                for blk in range(c // m):
                    rs = slice(blk * m, (blk + 1) * m)
                    is_upper = blk % 2 == 1
                    q_parts.append(qb[rs] * dl[rs] if is_upper else zero)
                    k_parts.append(zero if is_upper else kb[rs] * dl[rs])
                qt = jnp.concatenate(q_parts, axis=0)
                kt = jnp.concatenate(k_parts, axis=0)
            else:
                fine = li - HGRN_COARSE
                qt = qb * dl * rmask_ref[0, fine]
                kt = kb * dl * rmask_ref[1, fine]
            level = _dot(qt, kt, NT_DIMS)
            scores = scores + (level if 2 * m == c else level * smask_ref[li])
        q_in = (qh * b_in[:, sl]).astype(BF16)
        k_out = (kh * b_out[:, sl]).astype(BF16)
        state_t = state_ref[h]
        o = _dot(scores.astype(BF16), vh) + _dot(q_in, state_t.astype(BF16), NT_DIMS)
        state_ref[h] = state_t * chunk_decay[:, sl] + _dot(vh, k_out, TN_DIMS)
        gh = g_ref[:, sl]
        o_ref[:, sl] = (_rms(o, 1e-6) * ng_ref[:, sl] * _silu(gh)).astype(BF16)


def _attn_slope(h):
    return 2.0 ** (-8.0 * (h + 1) / ATTN_Q_HEADS)


def _attn_init_bias(bias_ref):
    c = ATTN_BLOCK
    qi = lax.broadcasted_iota(jnp.int32, (c, 2 * c), 0)
    kj = lax.broadcasted_iota(jnp.int32, (c, 2 * c), 1)
    dist = qi + c - kj
    in_window = (dist >= 0) & (dist < c)
    dist_f = dist.astype(F32)
    for h in range(ATTN_Q_HEADS):
        alibi = -_attn_slope(h) * dist_f
        bias_ref[0, h] = jnp.where(in_window & (kj >= c), alibi, NEG_BIG)
        bias_ref[1, h] = jnp.where(in_window, alibi, NEG_BIG)


def _attn_body(sink_ref, q_ref, k_ref, v_ref, o_ref, prev_ref, bias_ref):
    c = ATTN_BLOCK
    kvw = ATTN_KV_HEADS * ATTN_DIM
    which = jnp.where(pl.program_id(0) == 0, 0, 1)

    k_all =jnp.concatenate([prev_ref[:, :kvw], k_ref[...]], axis=0).astype(BF16)
    v_all = jnp.concatenate([prev_ref[:, kvw:], v_ref[...]], axis=0).astype(BF16)
    group = ATTN_Q_HEADS // ATTN_KV_HEADS
    outs = []
    for h in range(ATTN_Q_HEADS):
        g = h // group
        qh = (q_ref[:, h * ATTN_DIM:(h + 1) * ATTN_DIM] * (ATTN_DIM ** -0.5)).astype(BF16)
        kg = k_all[:, g * ATTN_DIM:(g + 1) * ATTN_DIM]
        vg = v_all[:, g * ATTN_DIM:(g + 1) * ATTN_DIM]
        logits = _dot(qh, kg, NT_DIMS) + bias_ref[which, h]
        sink = sink_ref[h]
        mx = jnp.maximum(jnp.max(logits, axis=-1, keepdims=True), sink)
        p = jnp.exp(logits - mx)
        denom = jnp.sum(p, axis=-1, keepdims=True) + jnp.exp(sink - mx)
        outs.append(_dot(p.astype(BF16), vg) / denom)
    o_ref[...] = jnp.concatenate(outs, axis=1).astype(BF16)
    prev_ref[:, :kvw] = k_ref[...]
    prev_ref[:, kvw:] = v_ref[...]


def _ssd_body(xs_ref, sb_ref, sc_ref, z_ref, dt_ref, cw_ref, cb_ref, dtb_ref, alog_ref,
              dskip_ref, ng_ref, expand_ref, o_ref, prev_ref, state_ref):
    c = SSD_CHUNK
    width = SSD_HEADS * SSD_DIM
    gw = SSD_GROUPS * SSD_STATE
    per_group = SSD_HEADS // SSD_GROUPS

    def conv_silu(x_ref, lo, hi):
        x = x_ref[...]
        ext = jnp.concatenate([prev_ref[:, lo:hi], x], axis=0)
        conv = x * cw_ref[SSD_CONV - 1:SSD_CONV, lo:hi] + cb_ref[:, lo:hi]
        for back in range(1, SSD_CONV):
            shifted = ext[SSD_TAIL - back:SSD_TAIL - back + c]
            conv = conv + shifted * cw_ref[SSD_CONV - 1 - back:SSD_CONV - back, lo:hi]
        prev_ref[:, lo:hi] = x[c - SSD_TAIL:]
        return _silu(conv)

    xs = conv_silu(xs_ref, 0, width)
    bm = conv_silu(sb_ref, width, width + gw).astype(BF16)
    cm = conv_silu(sc_ref, width + gw, width + 2 * gw).astype(BF16)

    dt_in = dt_ref[...] + dtb_ref[...]
    dt = jnp.maximum(dt_in, 0.0) + jnp.log1p(jnp.exp(-jnp.abs(dt_in)))
    b = _cumsum_rows(dt * (-jnp.exp(alog_ref[...])))
    b_t = b.T
    wide = _dot(_split3(jnp.concatenate([dt, b], axis=0), 1), expand_ref[...])
    dt_w, b_w = wide[:c], wide[c:]
    b_last = b_w[c - 1:c, :]
    v = xs * dt_w
    v_bf = v.astype(BF16)
    v_out = (v * jnp.exp(b_last - b_w)).astype(BF16)
    in_scale = jnp.exp(b_w)

    ti = lax.broadcasted_iota(jnp.int32, (c, c), 0)
    si = lax.broadcasted_iota(jnp.int32, (c, c), 1)
    causal = ti >= si
    lane = lax.broadcasted_iota(jnp.int32, (c, 2 * SSD_DIM), 1)
    low = lane < SSD_DIM

    pieces = []
    for g in range(SSD_GROUPS):
        cg = cm[:, g * SSD_STATE:(g + 1) * SSD_STATE]
        bg = bm[:, g * SSD_STATE:(g + 1) * SSD_STATE]
        cb = _dot(cg, bg, NT_DIMS)
        gs = slice(g * per_group * SSD_DIM, (g + 1) * per_group * SSD_DIM)
        state = state_ref[g]
        inter = _dot(cg, state.astype(BF16)) * in_scale[:, gs]
        state_ref[g] = state * jnp.exp(b_last[:, gs]) + _dot(bg, v_out[:, gs], TN_DIMS)
        for pair in range(per_group // 2):
            p_idx = g * (per_group // 2) + pair
            probs = []
            for h in (2 * p_idx, 2 * p_idx + 1):
                w = jnp.exp(jnp.where(causal, b[:, h:h + 1] - b_t[h:h + 1, :], NEG_BIG))
                probs.append((cb * w).astype(BF16))
            v_pair = v_bf[:, p_idx * 2 * SSD_DIM:(p_idx + 1) * 2 * SSD_DIM]
            v_diag = jnp.concatenate([jnp.where(low, v_pair, jnp.zeros_like(v_pair)),
                                      jnp.where(low, jnp.zeros_like(v_pair), v_pair)], axis=0)
            intra = _dot(jnp.concatenate(probs, axis=1), v_diag)
            pieces.append(intra + inter[:, pair * 2 * SSD_DIM:(pair + 1) * 2 * SSD_DIM])
    y = jnp.concatenate(pieces, axis=1) + dskip_ref[...] * xs
    y = y * _silu(z_ref[...])
    o_ref[...] = (_rms(y, 1e-6) * ng_ref[...]).astype(BF16)


def _ssd_expand_matrix():
    width = SSD_HEADS * SSD_DIM
    expand = np.zeros((LANES, width), np.float32)
    for h in range(SSD_HEADS):
        expand[h, h * SSD_DIM:(h + 1) * SSD_DIM] = 1.0
    return np.concatenate([expand] * 3, axis=0)


def _ret_log_gamma(h):
    return math.log(1.0 - 2.0 ** (-5.0 - h))


def _ret_init_mask(mask_ref):
    c = RET_CHUNK
    ti = lax.broadcasted_iota(jnp.int32, (c, c), 0)
    si = lax.broadcasted_iota(jnp.int32, (c, c), 1)
    lag = (ti - si).astype(F32)
    for h in range(RET_HEADS):
        mask_ref[h] = jnp.where(ti >= si, jnp.exp(lag * _ret_log_gamma(h)), 0.0)


def _ret_body(q_ref, k_ref, v_ref, g_ref, o_ref, state_ref, mask_ref):
    c = RET_CHUNK
    pos =lax.broadcasted_iota(jnp.int32, (c, 1), 0).astype(F32)
    for h in range(RET_HEADS):
        lg = _ret_log_gamma(h)
        qh = q_ref[:, h * RET_QK:(h + 1) * RET_QK]
        kh = k_ref[:, h * RET_QK:(h + 1) * RET_QK] * (RET_QK ** -0.5)
        vh = v_ref[:, h * RET_V:(h + 1) * RET_V].astype(BF16)
        scores = _dot(qh.astype(BF16), kh.astype(BF16), NT_DIMS) * mask_ref[h]
        q_in = (qh * jnp.exp((pos + 1.0) * lg)).astype(BF16)
        k_out = (kh * jnp.exp((c - 1.0 - pos) * lg)).astype(BF16)
        state_t = state_ref[h]
        o = _dot(scores.astype(BF16), vh) + _dot(q_in, state_t.astype(BF16), NT_DIMS)
        state_ref[h] = state_t * math.exp(c * lg) + _dot(vh, k_out, TN_DIMS)
        centered = o - jnp.mean(o, axis=-1, keepdims=True)
        normed = centered * lax.rsqrt(jnp.mean(centered * centered, axis=-1, keepdims=True) + 1e-5)
        gh = g_ref[:, h * RET_V:(h + 1) * RET_V]
        o_ref[:, h * RET_V:(h + 1) * RET_V] = (normed * _silu(gh)).astype(BF16)


MIX_ROWS = 128
assert HGRN_CHUNK == ATTN_BLOCK == SSD_CHUNK == RET_CHUNK == MIX_ROWS


def _mixers_kernel(sink_ref,
                   hq_ref, hf_ref, hi_ref, hg_ref, lb_ref, hng_ref, mat_ref,
                   aq_ref, ak_ref, av_ref,
                   xs_ref, sb_ref, sc_ref, sz_ref, dt_ref, cw_ref, cb_ref, dtb_ref, alog_ref,
                   dskip_ref, sng_ref, expand_ref,
                   rq_ref, rk_ref, rv_ref, rg_ref,
                   oh_ref, oa_ref, os_ref, or_ref,
                   hstate_ref, hsmask_ref, hrmask_ref, aprev_ref, abias_ref, sprev_ref, sstate_ref,
                   rstate_ref, rmask_ref):
    @pl.when(pl.program_id(0) == 0)
    def _():
        for ref in (hstate_ref, aprev_ref, sprev_ref, sstate_ref, rstate_ref):
            ref[...] = jnp.zeros_like(ref)
        _hgrn_init_masks(hsmask_ref, hrmask_ref)
        _attn_init_bias(abias_ref)
        _ret_init_mask(rmask_ref)

    _hgrn_body(hq_ref, hf_ref, hi_ref, hg_ref, lb_ref, hng_ref, mat_ref, oh_ref, hstate_ref,
               hsmask_ref, hrmask_ref)
    _attn_body(sink_ref, aq_ref, ak_ref, av_ref, oa_ref, aprev_ref, abias_ref)
    _ssd_body(xs_ref, sb_ref, sc_ref, sz_ref, dt_ref, cw_ref, cb_ref, dtb_ref, alog_ref,
              dskip_ref, sng_ref, expand_ref, os_ref, sprev_ref, sstate_ref)
    _ret_body(rq_ref, rk_ref, rv_ref, rg_ref, or_ref, rstate_ref, rmask_ref)


def _mixers(proj, lower_bound, hgrn_norm_g, sinks, conv_w, conv_b, dt_bias, a_log, d_skip,
            ssd_norm_g):
    t = proj.shape[0]
    c = MIX_ROWS
    bw = BRANCH_WIDTH
    gw = SSD_GROUPS * SSD_STATE
    xw = bw + 2 * gw
    kvw = ATTN_KV_HEADS * ATTN_DIM
    qkw = RET_HEADS * RET_QK
    mat = jnp.asarray(_hgrn_exponent_matrix(), BF16)
    expand = jnp.asarray(_ssd_expand_matrix(), BF16)

    def seg(name, width):
        return pl.BlockSpec((c, width), lambda i, b=COL[name] // width: (i, b))

    def full(a):
        return pl.BlockSpec(a.shape, lambda i: (0, 0))

    def pad_lanes(a):
        return jnp.pad(a.reshape(1, -1), ((0, 0), (0, LANES - a.shape[-1])))

    params = [lower_bound.reshape(1, bw), hgrn_norm_g.reshape(1, bw), mat]
    ssd_params = [conv_w, conv_b.reshape(1, xw), pad_lanes(dt_bias), pad_lanes(a_log),
                  jnp.repeat(d_skip, SSD_DIM).reshape(1, bw), ssd_norm_g.reshape(1, bw), expand]
    in_specs = ([pl.BlockSpec(memory_space=pltpu.SMEM)]
                + [seg(n, bw) for n in ("hq", "hf", "hi", "hg")] + [full(a) for a in params]
                + [seg("aq", bw), seg("ak", kvw), seg("av", kvw)]
                + [seg("xs", bw), seg("sb", gw), seg("sc", gw), seg("sz", bw), seg("dt", LANES)]
                + [full(a) for a in ssd_params]
                + [seg("rq", qkw), seg("rk", qkw), seg("rv", bw), seg("rg", bw)])
    out_spec = pl.BlockSpec((c, bw), lambda i: (i, 0))
    return pl.pallas_call(
        _mixers_kernel,
        grid=(t // c,),
        in_specs=in_specs,
        out_specs=[out_spec] * 4,
        out_shape=[jax.ShapeDtypeStruct((t, bw), BF16)] * 4,
        scratch_shapes=[pltpu.VMEM((HGRN_HEADS, HGRN_DIM, HGRN_DIM), F32),
                        pltpu.VMEM((len(HGRN_SIZES) + 1, c, c), F32),
                        pltpu.VMEM((2, len(HGRN_SIZES) - HGRN_COARSE, c, HGRN_DIM), BF16),
                        pltpu.VMEM((c, 2 * kvw), F32),
                        pltpu.VMEM((2, ATTN_Q_HEADS, c, 2 * c), F32),
                        pltpu.VMEM((SSD_TAIL, xw), F32),
                        pltpu.VMEM((SSD_GROUPS, SSD_STATE, bw // SSD_GROUPS), F32),
                        pltpu.VMEM((RET_HEADS, RET_V, RET_QK), F32),
                        pltpu.VMEM((RET_HEADS, c, c), F32)],
        compiler_params=_params("arbitrary"),
        name="mixers",
    )(sinks, *([proj] * 4), *params, *([proj] * 3), *([proj] * 5), *ssd_params, *([proj] * 4))


def kernel(x, norm_mix_pre, norm_mix_post, norm_ffn_pre, norm_ffn_post, w_in,
           hgrn_lb_logits, hgrn_norm, attn_sinks, ssd_conv_w, ssd_conv_b, ssd_dt_bias,
           ssd_a_log, ssd_d, ssd_norm, w_gate_up, b_gate, w_branch, w_out,
           w_ffn_gate, w_ffn_up, w_ffn_down):
    b_, t_, d_ = x.shape
    xr = x.reshape(b_ * t_, d_)
    lower_bounds = _lower_bounds(hgrn_lb_logits)
    h = _norm(xr, norm_mix_pre[0])
    for l in range(DEPTH):
        proj = _in_proj(h, w_in, l)
        outs = _mixers(proj, lower_bounds[l], hgrn_norm[l], attn_sinks[l], ssd_conv_w[l],
                       ssd_conv_b[l], ssd_dt_bias[l], ssd_a_log[l], ssd_d[l], ssd_norm[l])
        merged = _merge(outs, proj, w_branch, w_gate_up, b_gate, l)
        y = _rows_resident_matmul(merged, w_out, l, F32, name="out_proj")
        xr, h = _resid_norm(xr, y, norm_mix_post[l], norm_ffn_pre[l])
        u = _gate_up(h, w_ffn_gate, w_ffn_up, l)
        ff = _down(u, w_ffn_down, l)
        g_next = norm_mix_pre[l + 1] if l + 1 < DEPTH else None
        xr, h = _resid_norm(xr, ff, norm_ffn_post[l], g_next)
    return xr.reshape(b_, t_, d_)
```

```python
import functools
import math

import numpy as np
import jax
import jax.numpy as jnp
from jax import lax
from jax.experimental import pallas as pl
from jax.experimental.pallas import tpu as pltpu

F32 = jnp.float32
BF16 = jnp.bfloat16

D_MODEL = 4096
DEPTH = 4
BRANCH_WIDTH = D_MODEL // 4
HGRN_HEADS, HGRN_DIM, HGRN_CHUNK = 8, 128, 128
HGRN_HALF = HGRN_CHUNK // 2
ATTN_Q_HEADS, ATTN_KV_HEADS, ATTN_DIM, ATTN_BLOCK = 16, 4, 64, 128
SSD_HEADS, SSD_DIM, SSD_GROUPS, SSD_STATE, SSD_CONV, SSD_CHUNK = 16, 64, 4, 128, 4, 128
SSD_TAIL = 8
RET_HEADS, RET_QK, RET_V, RET_CHUNK = 8, 64, 128, 128
GATE_RANK = 256
FFN_HIDDEN = -(-8 * D_MODEL // (3 * 256)) * 256
FFN_BLOCK = 256
FFN_KBLOCK = 1024
IN_WIDTH = 12048

LANES = 128
BF16_ROWS = 16
VMEM_LIMIT = 56 * 1024 * 1024

IN_BLOCK = 256
_FIRST_ORDER = list(range(20)) + list(range(22, 30)) + [20, 21] + list(range(30, 34))
PROJ_SRC = ([9744 + IN_BLOCK * i for i in range(4)] + [10768 + IN_BLOCK * i for i in range(4)]
            + [IN_BLOCK * j for j in _FIRST_ORDER]
            + [8720, 8720 + IN_BLOCK, 9232, 9232 + IN_BLOCK, 11792, 8704])
PROJ_WIDTH = IN_BLOCK * len(PROJ_SRC)
assert set(range(IN_WIDTH)) == {s + i for s in PROJ_SRC for i in range(IN_BLOCK)}
COL = {"rv": 0, "rg": 1024, "hq": 2048, "hf": 3072, "hi": 4096, "hg": 5120, "aq": 6144,
       "sz": 7168, "xs": 8192, "ak": 9216, "av": 9472, "sb": 9728, "sc": 10240,
       "rq": 10752, "rk": 11264, "gc": 11776, "dt": 12032}


NT_DIMS = (((1,), (1,)), ((), ()))
TN_DIMS = (((0,), (0,)), ((), ()))
NEG_BIG = -1e30
LOG2_E = 1.4426950408889634


def _sigmoid(x):
    return 1.0 / (1.0 + jnp.exp2(x * (-LOG2_E)))


def _silu(x):
    return x * _sigmoid(x)


def _rms(x, eps):
    return x * lax.rsqrt(jnp.mean(x * x, axis=-1, keepdims=True) + eps)


def _dot(a, b, dims=None):
    if dims is None:
        dims = (((a.ndim - 1,), (0,)), ((), ()))
    return lax.dot_general(a, b, dims, preferred_element_type=F32)


def _split3(x, axis):
    hi = x.astype(BF16)
    rest = x - hi.astype(F32)
    mid = rest.astype(BF16)
    lo = (rest - mid.astype(F32)).astype(BF16)
    return jnp.concatenate([hi, mid, lo], axis=axis)


def _cumsum_rows(x):
    n = x.shape[0]
    row = lax.broadcasted_iota(jnp.int32, x.shape, 0)
    k = 1
    while k < n:
        x = x + jnp.where(row >= k, pltpu.roll(x, k, 0), 0.0)
        k *= 2
    return x


def _params(*sem):
    return pltpu.CompilerParams(dimension_semantics=sem, vmem_limit_bytes=VMEM_LIMIT)


def _rows_resident_kernel(a_ref, w0_ref, w1_ref, o_ref):
    a = a_ref[...]
    tn = w0_ref.shape[1]
    o_ref[:, :tn] = _dot(a, w0_ref[...].astype(BF16)).astype(o_ref.dtype)
    o_ref[:, tn:] = _dot(a, w1_ref[...].astype(BF16)).astype(o_ref.dtype)


def _rows_resident_matmul(a, w, layer, out_dtype, tm=2048, tn=IN_BLOCK, name="matmul"):
    m, kd = a.shape
    n = w.shape[2]
    return pl.pallas_call(
        _rows_resident_kernel,
        grid=(m // tm, n // (2 * tn)),
        in_specs=[pl.BlockSpec((tm, kd), lambda i, j: (i, 0), pipeline_mode=pl.Buffered(1))]
        + [pl.BlockSpec((None, kd, tn), lambda i, j, k=k: (layer, 0, 2 * j + k)) for k in range(2)],
        out_specs=pl.BlockSpec((tm, 2 * tn), lambda i, j: (i, j)),
        out_shape=jax.ShapeDtypeStruct((m, n), out_dtype),
        compiler_params=_params("parallel", "arbitrary"),
        name=name,
    )(a, w, w)


def _in_proj_kernel(src_ref, a_ref, wt0_ref, wt1_ref, o_ref):
    del src_ref
    a = a_ref[...]
    o_ref[:, :IN_BLOCK] = _dot(a, wt0_ref[...].astype(BF16), NT_DIMS)
    o_ref[:, IN_BLOCK:] = _dot(a, wt1_ref[...].astype(BF16), NT_DIMS)


def _in_proj(h, w_in, layer, tm=2048):
    t, d = h.shape
    w_t = jnp.swapaxes(w_in, 1, 2)
    src = jnp.asarray(PROJ_SRC, jnp.int32)
    grid_spec = pltpu.PrefetchScalarGridSpec(
        num_scalar_prefetch=1,
        grid=(t // tm, len(PROJ_SRC) // 2),
        in_specs=[pl.BlockSpec((tm, d), lambda i, p, src: (i, 0), pipeline_mode=pl.Buffered(1))]
        + [pl.BlockSpec((None, pl.Element(IN_BLOCK), pl.Element(d)),
                        lambda i, p, src, k=k: (layer, pl.multiple_of(src[2 * p + k], 8), 0))
           for k in range(2)],
        out_specs=pl.BlockSpec((tm, 2 * IN_BLOCK), lambda i, p, src: (i, p)),
    )
    return pl.pallas_call(
        _in_proj_kernel,
        grid_spec=grid_spec,
        out_shape=jax.ShapeDtypeStruct((t, PROJ_WIDTH), F32),
        compiler_params=_params("parallel", "arbitrary"),
        name="in_proj",
    )(src, h, w_t, w_t)


def _norm_kernel(x_ref, g_ref, h_ref):
    h_ref[...] = (_rms(x_ref[...], 1e-6) * g_ref[...]).astype(BF16)


def _norm(x, g, rows=256):
    t, d = x.shape
    return pl.pallas_call(
        _norm_kernel,
        grid=(t // rows,),
        in_specs=[pl.BlockSpec((rows, d), lambda i: (i, 0)),
                  pl.BlockSpec((1, d), lambda i: (0, 0))],
        out_specs=pl.BlockSpec((rows, d), lambda i: (i, 0)),
        out_shape=jax.ShapeDtypeStruct((t, d), BF16),
        compiler_params=_params("parallel"),
        name="norm",
    )(x, g.reshape(1, d))


def _resid_norm_kernel(x_ref, y_ref, gpost_ref, gnext_ref, xo_ref, h_ref):
    xn = x_ref[...] + _rms(y_ref[...], 1e-6) * gpost_ref[...]
    xo_ref[...] = xn
    h_ref[...] = (_rms(xn, 1e-6) * gnext_ref[...]).astype(BF16)


def _resid_kernel(x_ref, y_ref, gpost_ref, xo_ref):
    xo_ref[...] = x_ref[...] + _rms(y_ref[...], 1e-6) * gpost_ref[...]


def _resid_norm(x, y, g_post, g_next, rows=256):
    t, d = x.shape
    row_spec = pl.BlockSpec((rows, d), lambda i: (i, 0))
    vec_spec = pl.BlockSpec((1, d), lambda i: (0, 0))
    if g_next is None:
        return pl.pallas_call(
            _resid_kernel, grid=(t // rows,),
            in_specs=[row_spec, row_spec, vec_spec],
            out_specs=row_spec,
            out_shape=jax.ShapeDtypeStruct((t, d), F32),
            compiler_params=_params("parallel"),
            name="resid",
        )(x, y, g_post.reshape(1, d)), None
    return pl.pallas_call(
        _resid_norm_kernel, grid=(t // rows,),
        in_specs=[row_spec, row_spec, vec_spec, vec_spec],
        out_specs=[row_spec, row_spec],
        out_shape=[jax.ShapeDtypeStruct((t, d), F32), jax.ShapeDtypeStruct((t, d), BF16)],
        compiler_params=_params("parallel"),
        name="resid_norm",
    )(x, y, g_post.reshape(1, d), g_next.reshape(1, d))


def _merge_kernel(oa_ref, ob_ref, oc_ref, od_ref, gc_ref, wb_ref,
                  wg0_ref, wg1_ref, wg2_ref, wg3_ref, bg0_ref, bg1_ref, bg2_ref, bg3_ref,
                  out_ref):
    gc = gc_ref[...].astype(BF16)
    branches = ((oa_ref, wg0_ref, bg0_ref), (ob_ref, wg1_ref, bg1_ref),
                (oc_ref, wg2_ref, bg2_ref), (od_ref, wg3_ref, bg3_ref))
    acc = None
    for br, (o_ref, wg_ref, bg_ref) in enumerate(branches):
        y = _dot(o_ref[...], wb_ref[br].astype(BF16))
        gate = _sigmoid(_dot(gc, wg_ref[...].astype(BF16)) + bg_ref[...])
        acc = gate * y if acc is None else acc + gate * y
    out_ref[...] = acc.astype(BF16)


def _merge(outs, proj, w_branch, w_gate_up, b_gate, layer, tm=2048, tn=256):
    t = proj.shape[0]
    d = D_MODEL
    nj = d // tn
    once = pl.Buffered(1)
    o_spec = pl.BlockSpec((tm, BRANCH_WIDTH), lambda i, j: (i, 0))
    gc_spec = pl.BlockSpec((tm, GATE_RANK), lambda i, j: (i, COL["gc"] // GATE_RANK),
                           pipeline_mode=once)
    wb_spec = pl.BlockSpec((None, 4, BRANCH_WIDTH, tn), lambda i, j: (layer, 0, 0, j))
    wg_specs = [pl.BlockSpec((None, GATE_RANK, tn), lambda i, j, br=br: (layer, 0, br * nj + j))
                for br in range(4)]
    bg_specs = [pl.BlockSpec((None, 1, tn), lambda i, j, br=br: (layer, 0, br * nj + j))
                for br in range(4)]
    bg = b_gate.reshape(DEPTH, 1, 4 * d)
    return pl.pallas_call(
        _merge_kernel,
        grid=(t // tm, nj),
        in_specs=[o_spec] * 4 + [gc_spec, wb_spec] + wg_specs + bg_specs,
        out_specs=pl.BlockSpec((tm, tn), lambda i, j: (i, j)),
        out_shape=jax.ShapeDtypeStruct((t, d), BF16),
        compiler_params=_params("parallel", "arbitrary"),
        name="merge",
    )(*outs, proj, w_branch, *([w_gate_up] * 4), *([bg] * 4))


def _gate_up_kernel(h_ref, wg_ref, wu_ref, o_ref):
    h = h_ref[...]
    gate = _dot(h, wg_ref[...].astype(BF16))
    up = _dot(h, wu_ref[...].astype(BF16))
    o_ref[...] = (_silu(gate) * up).astype(BF16)


def _gate_up(h, w_gate, w_up, layer, tm=2048, tn=FFN_BLOCK):
    t, d = h.shape
    w_spec = pl.BlockSpec((None, d, tn), lambda i, j: (layer, 0, j))
    return pl.pallas_call(
        _gate_up_kernel,
        grid=(t // tm, FFN_HIDDEN // tn),
        in_specs=[pl.BlockSpec((tm, d), lambda i, j: (i, 0)), w_spec, w_spec],
        out_specs=pl.BlockSpec((tm, tn), lambda i, j: (i, j)),
        out_shape=jax.ShapeDtypeStruct((t, FFN_HIDDEN), BF16),
        compiler_params=_params("parallel", "arbitrary"),
        name="ffn_gate_up",
    )(h, w_gate, w_up)


def _down_kernel(u_ref, w_ref, ut_ref, wt_ref, o_ref, *, n_main):
    k = pl.program_id(2)

    @pl.when(k == 0)
    def _():
        o_ref[...] = _dot(u_ref[...], w_ref[...].astype(BF16))

    @pl.when((k > 0) & (k < n_main))
    def _():
        o_ref[...] += _dot(u_ref[...], w_ref[...].astype(BF16))

    @pl.when(k == n_main)
    def _():
        o_ref[...] += _dot(ut_ref[...], wt_ref[...].astype(BF16))


def _down(u, w_down, layer, tm=2048, tn=1024, tk=FFN_KBLOCK):
    t = u.shape[0]
    d = w_down.shape[2]
    n_main = FFN_HIDDEN // tk
    tail = FFN_HIDDEN - n_main * tk
    return pl.pallas_call(
        functools.partial(_down_kernel, n_main=n_main),
        grid=(t // tm, d // tn, n_main + 1),
        in_specs=[pl.BlockSpec((tm, tk), lambda i, j, k: (i, jnp.minimum(k, n_main - 1))),
                  pl.BlockSpec((None, tk, tn), lambda i, j, k: (layer, jnp.minimum(k, n_main - 1), j)),
                  pl.BlockSpec((pl.Element(tm), pl.Element(tail)),
                               lambda i, j, k: (pl.multiple_of(i * tm, tm), n_main * tk)),
                  pl.BlockSpec((None, pl.Element(tail), pl.Element(tn)),
                               lambda i, j, k: (layer, n_main * tk, pl.multiple_of(j * tn, tn)))],
        out_specs=pl.BlockSpec((tm, tn), lambda i, j, k: (i, j)),
        out_shape=jax.ShapeDtypeStruct((t, d), F32),
        compiler_params=_params("parallel", "parallel", "arbitrary"),
        name="ffn_down",
    )(u, w_down, u, w_down)


def _lower_bound_kernel(l_ref, o_ref):
    lg = l_ref[...]
    e = jnp.exp(lg - jnp.max(lg, axis=0, keepdims=True))
    p = e / jnp.sum(e, axis=0, keepdims=True)
    run = p[0:1]
    o_ref[0:1, :] = run - p[0:1]
    for l in range(1, DEPTH):
        run = run + p[l:l + 1]
        o_ref[l:l + 1, :] = run - p[0:1]


def _lower_bounds(logits):
    return pl.pallas_call(
        _lower_bound_kernel,
        out_shape=jax.ShapeDtypeStruct(logits.shape, F32),
        name="hgrn_lower_bounds",
    )(logits)


HGRN_LEVELS = (32, 16, 8, 4, 2, 1)


def _hgrn_exponent_matrix():
    c = HGRN_HALF
    mat = np.zeros(((len(HGRN_LEVELS) + 2) * c, c), np.float32)
    for li, m in enumerate(HGRN_LEVELS):
        for t in range(c):
            r = (t // (2 * m)) * 2 * m + m - 1
            if (t // m) % 2 == 1:
                mat[li * c + t, r + 1:t + 1] = 1.0
            else:
                mat[li * c + t, t + 1:r + 1] = 1.0
    for t in range(c):
        mat[len(HGRN_LEVELS) * c + t, :t + 1] = 1.0
        mat[(len(HGRN_LEVELS) + 1) * c + t, t + 1:] = 1.0
    return np.concatenate([mat, mat, mat], axis=1)


HGRN_SIZES = (HGRN_HALF,) + HGRN_LEVELS
HGRN_COARSE = sum(1 for m in HGRN_SIZES if m % BF16_ROWS == 0)


def _hgrn_init_masks(smask_ref, rmask_ref):
    c = HGRN_CHUNK
    row = lax.broadcasted_iota(jnp.int32, (c, c), 0)
    col = lax.broadcasted_iota(jnp.int32, (c, c), 1)
    for li, m in enumerate(HGRN_SIZES):
        shift = m.bit_length()
        smask_ref[li] = jnp.where((row >> shift) == (col >> shift), 1.0, 0.0)
    smask_ref[len(HGRN_SIZES)] = jnp.where(row == col, 1.0, 0.0)
    rows = lax.broadcasted_iota(jnp.int32, (c, HGRN_DIM), 0)
    for j, m in enumerate(HGRN_SIZES[HGRN_COARSE:]):
        upper = ((rows >> (m.bit_length() - 1)) & 1) == 1
        rmask_ref[0, j] = jnp.where(upper, 1.0, 0.0).astype(BF16)
        rmask_ref[1, j] = jnp.where(upper, 0.0, 1.0).astype(BF16)


def _hgrn_body(q_ref, f_ref, i_ref, g_ref, lb_ref, ng_ref, mat_ref, o_ref, state_ref,
               smask_ref, rmask_ref):
    c = HGRN_CHUNK
    hc = HGRN_HALF
    nl = len(HGRN_LEVELS)

    lb = lb_ref[...]
    forget = lb + (1.0 - lb) * _sigmoid(f_ref[...])
    key = 1.0 - forget
    query = _silu(q_ref[...])
    log_f = jnp.log(forget)
    sums = [_dot(mat_ref[...], _split3(log_f[s * hc:(s + 1) * hc], 0)) for s in range(2)]

    def both(group):
        return jnp.concatenate([sm[group * hc:(group + 1) * hc] for sm in sums], axis=0)

    prefix = [sm[nl * hc:(nl + 1) * hc] for sm in sums]
    suffix = [sm[(nl + 1) * hc:(nl + 2) * hc] for sm in sums]
    total = [p[hc - 1:hc] for p in prefix]
    level_exp = [jnp.concatenate([suffix[0], prefix[1]], axis=0)] + [both(li) for li in range(nl)]
    b_in = jnp.exp(jnp.concatenate([prefix[0], prefix[1] + total[0]], axis=0))
    b_out = jnp.exp(jnp.concatenate([suffix[0] + total[1], suffix[1]], axis=0))
    chunk_decay = jnp.exp(total[0] + total[1])

    for h in range(HGRN_HEADS):
        sl = slice(h * HGRN_DIM, (h + 1) * HGRN_DIM)
        qh, kh = query[:, sl], key[:, sl]
        qb, kb = qh.astype(BF16), kh.astype(BF16)
        vh = i_ref[:, sl].astype(BF16)
        scores = _dot(qb, kb, NT_DIMS) * smask_ref[len(HGRN_SIZES)]
        for li, m in enumerate(HGRN_SIZES):
            dl = jnp.exp(level_exp[li][:, sl]).astype(BF16)
            if m % BF16_ROWS == 0:
                zero = jnp.zeros((m, HGRN_DIM), BF16)
                q_parts, k_parts = [], []
                for blk in range(c // m):
                    rs = slice(blk * m, (blk + 1) * m)
                    is_upper = blk % 2 == 1
                    q_parts.append(qb[rs] * dl[rs] if is_upper else zero)
                    k_parts.append(zero if is_upper else kb[rs] * dl[rs])
                qt = jnp.concatenate(q_parts, axis=0)
                kt = jnp.concatenate(k_parts, axis=0)
            else:
                fine = li - HGRN_COARSE
                qt = qb * dl * rmask_ref[0, fine]
                kt = kb * dl * rmask_ref[1, fine]
            level = _dot(qt, kt, NT_DIMS)
            scores = scores + (level if 2 * m == c else level * smask_ref[li])
        q_in = (qh * b_in[:, sl]).astype(BF16)
        k_out = (kh * b_out[:, sl]).astype(BF16)
        state_t = state_ref[h]
        o = _dot(scores.astype(BF16), vh) + _dot(q_in, state_t.astype(BF16), NT_DIMS)
        state_ref[h] = state_t * chunk_decay[:, sl] + _dot(vh, k_out, TN_DIMS)
        gh = g_ref[:, sl]
        o_ref[:, sl] = (_rms(o, 1e-6) * ng_ref[:, sl] * _silu(gh)).astype(BF16)


def _attn_slope(h):
    return 2.0 ** (-8.0 * (h + 1) / ATTN_Q_HEADS)


def _attn_init_bias(bias_ref):
    c = ATTN_BLOCK
    qi = lax.broadcasted_iota(jnp.int32, (c, 2 * c), 0)
    kj = lax.broadcasted_iota(jnp.int32, (c, 2 * c), 1)
    dist = qi + c - kj
    in_window = (dist >= 0) & (dist < c)
    dist_f = dist.astype(F32)
    for h in range(ATTN_Q_HEADS):
        alibi = -_attn_slope(h) * dist_f
        bias_ref[0, h] = jnp.where(in_window & (kj >= c), alibi, NEG_BIG)
        bias_ref[1, h] = jnp.where(in_window, alibi, NEG_BIG)


def _attn_body(sink_ref, q_ref, k_ref, v_ref, o_ref, prev_ref, bias_ref, first):
    c = ATTN_BLOCK
    kvw = ATTN_KV_HEADS * ATTN_DIM
    which = jnp.where(first, 0, 1)

    k_all =jnp.concatenate([prev_ref[:, :kvw], k_ref[...]], axis=0).astype(BF16)
    v_all = jnp.concatenate([prev_ref[:, kvw:], v_ref[...]], axis=0).astype(BF16)
    group = ATTN_Q_HEADS // ATTN_KV_HEADS
    outs = []
    for h in range(ATTN_Q_HEADS):
        g = h // group
        qh = (q_ref[:, h * ATTN_DIM:(h + 1) * ATTN_DIM] * (ATTN_DIM ** -0.5)).astype(BF16)
        kg = k_all[:, g * ATTN_DIM:(g + 1) * ATTN_DIM]
        vg = v_all[:, g * ATTN_DIM:(g + 1) * ATTN_DIM]
        logits = _dot(qh, kg, NT_DIMS) + bias_ref[which, h]
        sink = sink_ref[h]
        mx = jnp.maximum(jnp.max(logits, axis=-1, keepdims=True), sink)
        p = jnp.exp(logits - mx)
        denom = jnp.sum(p, axis=-1, keepdims=True) + jnp.exp(sink - mx)
        outs.append(_dot(p.astype(BF16), vg) / denom)
    o_ref[...] = jnp.concatenate(outs, axis=1).astype(BF16)
    prev_ref[:, :kvw] = k_ref[...]
    prev_ref[:, kvw:] = v_ref[...]


def _ssd_body(xs_ref, sb_ref, sc_ref, z_ref, dt_ref, cw_ref, cb_ref, dtb_ref, alog_ref,
              dskip_ref, ng_ref, expand_ref, o_ref, prev_ref, state_ref):
    c = SSD_CHUNK
    width = SSD_HEADS * SSD_DIM
    gw = SSD_GROUPS * SSD_STATE
    per_group = SSD_HEADS // SSD_GROUPS

    def conv_silu(x_ref, lo, hi):
        x = x_ref[...]
        ext = jnp.concatenate([prev_ref[:, lo:hi], x], axis=0)
        conv = x * cw_ref[SSD_CONV - 1:SSD_CONV, lo:hi] + cb_ref[:, lo:hi]
        for back in range(1, SSD_CONV):
            shifted = ext[SSD_TAIL - back:SSD_TAIL - back + c]
            conv = conv + shifted * cw_ref[SSD_CONV - 1 - back:SSD_CONV - back, lo:hi]
        prev_ref[:, lo:hi] = x[c - SSD_TAIL:]
        return _silu(conv)

    xs = conv_silu(xs_ref, 0, width)
    bm = conv_silu(sb_ref, width, width + gw).astype(BF16)
    cm = conv_silu(sc_ref, width + gw, width + 2 * gw).astype(BF16)

    dt_in = dt_ref[...] + dtb_ref[...]
    dt = jnp.maximum(dt_in, 0.0) + jnp.log1p(jnp.exp(-jnp.abs(dt_in)))
    b = _cumsum_rows(dt * (-jnp.exp(alog_ref[...])))
    b_t = b.T
    wide = _dot(_split3(jnp.concatenate([dt, b], axis=0), 1), expand_ref[...])
    dt_w, b_w = wide[:c], wide[c:]
    b_last = b_w[c - 1:c, :]
    v = xs * dt_w
    v_bf = v.astype(BF16)
    v_out = (v * jnp.exp(b_last - b_w)).astype(BF16)
    in_scale = jnp.exp(b_w)

    ti = lax.broadcasted_iota(jnp.int32, (c, c), 0)
    si = lax.broadcasted_iota(jnp.int32, (c, c), 1)
    causal = ti >= si
    lane = lax.broadcasted_iota(jnp.int32, (c, 2 * SSD_DIM), 1)
    low = lane < SSD_DIM

    pieces = []
    for g in range(SSD_GROUPS):
        cg = cm[:, g * SSD_STATE:(g + 1) * SSD_STATE]
        bg = bm[:, g * SSD_STATE:(g + 1) * SSD_STATE]
        cb = _dot(cg, bg, NT_DIMS)
        gs = slice(g * per_group * SSD_DIM, (g + 1) * per_group * SSD_DIM)
        state = state_ref[g]
        inter = _dot(cg, state.astype(BF16)) * in_scale[:, gs]
        state_ref[g] = state * jnp.exp(b_last[:, gs]) + _dot(bg, v_out[:, gs], TN_DIMS)
        for pair in range(per_group // 2):
            p_idx = g * (per_group // 2) + pair
            probs = []
            for h in (2 * p_idx, 2 * p_idx + 1):
                w = jnp.exp(jnp.where(causal, b[:, h:h + 1] - b_t[h:h + 1, :], NEG_BIG))
                probs.append((cb * w).astype(BF16))
            v_pair = v_bf[:, p_idx * 2 * SSD_DIM:(p_idx + 1) * 2 * SSD_DIM]
            v_diag = jnp.concatenate([jnp.where(low, v_pair, jnp.zeros_like(v_pair)),
                                      jnp.where(low, jnp.zeros_like(v_pair), v_pair)], axis=0)
            intra = _dot(jnp.concatenate(probs, axis=1), v_diag)
            pieces.append(intra + inter[:, pair * 2 * SSD_DIM:(pair + 1) * 2 * SSD_DIM])
    y = jnp.concatenate(pieces, axis=1) + dskip_ref[...] * xs
    y = y * _silu(z_ref[...])
    o_ref[...] = (_rms(y, 1e-6) * ng_ref[...]).astype(BF16)


def _ssd_expand_matrix():
    width = SSD_HEADS * SSD_DIM
    expand = np.zeros((LANES, width), np.float32)
    for h in range(SSD_HEADS):
        expand[h, h * SSD_DIM:(h + 1) * SSD_DIM] = 1.0
    return np.concatenate([expand] * 3, axis=0)


def _ret_log_gamma(h):
    return math.log(1.0 - 2.0 ** (-5.0 - h))


def _ret_init_mask(mask_ref):
    c = RET_CHUNK
    ti = lax.broadcasted_iota(jnp.int32, (c, c), 0)
    si = lax.broadcasted_iota(jnp.int32, (c, c), 1)
    lag = (ti - si).astype(F32)
    for h in range(RET_HEADS):
        mask_ref[h] = jnp.where(ti >= si, jnp.exp(lag * _ret_log_gamma(h)), 0.0)


def _ret_body(q_ref, k_ref, v_ref, g_ref, o_ref, state_ref, mask_ref):
    c = RET_CHUNK
    pos =lax.broadcasted_iota(jnp.int32, (c, 1), 0).astype(F32)
    for h in range(RET_HEADS):
        lg = _ret_log_gamma(h)
        qh = q_ref[:, h * RET_QK:(h + 1) * RET_QK]
        kh = k_ref[:, h * RET_QK:(h + 1) * RET_QK] * (RET_QK ** -0.5)
        vh = v_ref[:, h * RET_V:(h + 1) * RET_V].astype(BF16)
        scores = _dot(qh.astype(BF16), kh.astype(BF16), NT_DIMS) * mask_ref[h]
        q_in = (qh * jnp.exp((pos + 1.0) * lg)).astype(BF16)
        k_out = (kh * jnp.exp((c - 1.0 - pos) * lg)).astype(BF16)
        state_t = state_ref[h]
        o = _dot(scores.astype(BF16), vh) + _dot(q_in, state_t.astype(BF16), NT_DIMS)
        state_ref[h] = state_t * math.exp(c * lg) + _dot(vh, k_out, TN_DIMS)
        centered = o - jnp.mean(o, axis=-1, keepdims=True)
        normed = centered * lax.rsqrt(jnp.mean(centered * centered, axis=-1, keepdims=True) + 1e-5)
        gh = g_ref[:, h * RET_V:(h + 1) * RET_V]
        o_ref[:, h * RET_V:(h + 1) * RET_V] = (normed * _silu(gh)).astype(BF16)


MIX_ROWS = 128
MIX_STEP_CHUNKS = 2
assert HGRN_CHUNK == ATTN_BLOCK == SSD_CHUNK == RET_CHUNK == MIX_ROWS


def _mixers_kernel(sink_ref,
                   hq_ref, hf_ref, hi_ref, hg_ref, lb_ref, hng_ref, mat_ref,
                   aq_ref, ak_ref, av_ref,
                   xs_ref, sb_ref, sc_ref, sz_ref, dt_ref, cw_ref, cb_ref, dtb_ref, alog_ref,
                   dskip_ref, sng_ref, expand_ref,
                   rq_ref, rk_ref, rv_ref, rg_ref,
                   oh_ref, oa_ref, os_ref, or_ref,
                   hstate_ref, hsmask_ref, hrmask_ref, aprev_ref, abias_ref, sprev_ref, sstate_ref,
                   rstate_ref, rmask_ref):
    @pl.when(pl.program_id(0) == 0)
    def _():
        for ref in (hstate_ref, aprev_ref, sprev_ref, sstate_ref, rstate_ref):
            ref[...] = jnp.zeros_like(ref)
        _hgrn_init_masks(hsmask_ref, hrmask_ref)
        _attn_init_bias(abias_ref)
        _ret_init_mask(rmask_ref)

    for sub in range(MIX_STEP_CHUNKS):
        def rows(ref, sub=sub):
            return ref.at[pl.ds(sub * MIX_ROWS, MIX_ROWS)]

        first = jnp.logical_and(pl.program_id(0) == 0, sub == 0)
        _hgrn_body(rows(hq_ref), rows(hf_ref), rows(hi_ref), rows(hg_ref), lb_ref, hng_ref, mat_ref,
                   rows(oh_ref), hstate_ref, hsmask_ref, hrmask_ref)
        _attn_body(sink_ref, rows(aq_ref), rows(ak_ref), rows(av_ref), rows(oa_ref), aprev_ref,
                   abias_ref, first)
        _ssd_body(rows(xs_ref), rows(sb_ref), rows(sc_ref), rows(sz_ref), rows(dt_ref), cw_ref,
                  cb_ref, dtb_ref, alog_ref, dskip_ref, sng_ref, expand_ref, rows(os_ref),
                  sprev_ref, sstate_ref)
        _ret_body(rows(rq_ref), rows(rk_ref), rows(rv_ref), rows(rg_ref), rows(or_ref),
                  rstate_ref, rmask_ref)


def _mixers(proj, lower_bound, hgrn_norm_g, sinks, conv_w, conv_b, dt_bias, a_log, d_skip,
            ssd_norm_g):
    t = proj.shape[0]
    c = MIX_ROWS
    bw = BRANCH_WIDTH
    gw = SSD_GROUPS * SSD_STATE
    xw = bw + 2 * gw
    kvw = ATTN_KV_HEADS * ATTN_DIM
    qkw = RET_HEADS * RET_QK
    mat = jnp.asarray(_hgrn_exponent_matrix(), BF16)
    expand = jnp.asarray(_ssd_expand_matrix(), BF16)

    step = MIX_STEP_CHUNKS * c

    def seg(name, width):
        return pl.BlockSpec((step, width), lambda i, b=COL[name] // width: (i, b))

    def full(a):
        return pl.BlockSpec(a.shape, lambda i: (0, 0))

    def pad_lanes(a):
        return jnp.pad(a.reshape(1, -1), ((0, 0), (0, LANES - a.shape[-1])))

    params = [lower_bound.reshape(1, bw), hgrn_norm_g.reshape(1, bw), mat]
    ssd_params = [conv_w, conv_b.reshape(1, xw), pad_lanes(dt_bias), pad_lanes(a_log),
                  jnp.repeat(d_skip, SSD_DIM).reshape(1, bw), ssd_norm_g.reshape(1, bw), expand]
    in_specs = ([pl.BlockSpec(memory_space=pltpu.SMEM)]
                + [seg(n, bw) for n in ("hq", "hf", "hi", "hg")] + [full(a) for a in params]
                + [seg("aq", bw), seg("ak", kvw), seg("av", kvw)]
                + [seg("xs", bw), seg("sb", gw), seg("sc", gw), seg("sz", bw), seg("dt", LANES)]
                + [full(a) for a in ssd_params]
                + [seg("rq", qkw), seg("rk", qkw), seg("rv", bw), seg("rg", bw)])
    out_spec = pl.BlockSpec((step, bw), lambda i: (i, 0))
    return pl.pallas_call(
        _mixers_kernel,
        grid=(t // step,),
        in_specs=in_specs,
        out_specs=[out_spec] * 4,
        out_shape=[jax.ShapeDtypeStruct((t, bw), BF16)] * 4,
        scratch_shapes=[pltpu.VMEM((HGRN_HEADS, HGRN_DIM, HGRN_DIM), F32),
                        pltpu.VMEM((len(HGRN_SIZES) + 1, c, c), F32),
                        pltpu.VMEM((2, len(HGRN_SIZES) - HGRN_COARSE, c, HGRN_DIM), BF16),
                        pltpu.VMEM((c, 2 * kvw), F32),
                        pltpu.VMEM((2, ATTN_Q_HEADS, c, 2 * c), F32),
                        pltpu.VMEM((SSD_TAIL, xw), F32),
                        pltpu.VMEM((SSD_GROUPS, SSD_STATE, bw // SSD_GROUPS), F32),
                        pltpu.VMEM((RET_HEADS, RET_V, RET_QK), F32),
                        pltpu.VMEM((RET_HEADS, c, c), F32)],
        compiler_params=_params("arbitrary"),
        name="mixers",
    )(sinks, *([proj] * 4), *params, *([proj] * 3), *([proj] * 5), *ssd_params, *([proj] * 4))


def kernel(x, norm_mix_pre, norm_mix_post, norm_ffn_pre, norm_ffn_post, w_in,
           hgrn_lb_logits, hgrn_norm, attn_sinks, ssd_conv_w, ssd_conv_b, ssd_dt_bias,
           ssd_a_log, ssd_d, ssd_norm, w_gate_up, b_gate, w_branch, w_out,
           w_ffn_gate, w_ffn_up, w_ffn_down):
    b_, t_, d_ = x.shape
    xr = x.reshape(b_ * t_, d_)
    lower_bounds = _lower_bounds(hgrn_lb_logits)
    h = _norm(xr, norm_mix_pre[0])
    for l in range(DEPTH):
        proj = _in_proj(h, w_in, l)
        outs = _mixers(proj, lower_bounds[l], hgrn_norm[l], attn_sinks[l], ssd_conv_w[l],
                       ssd_conv_b[l], ssd_dt_bias[l], ssd_a_log[l], ssd_d[l], ssd_norm[l])
        merged = _merge(outs, proj, w_branch, w_gate_up, b_gate, l)
        y = _rows_resident_matmul(merged, w_out, l, F32, name="out_proj")
        xr, h = _resid_norm(xr, y, norm_mix_post[l], norm_ffn_pre[l])
        u = _gate_up(h, w_ffn_gate, w_ffn_up, l)
        ff = _down(u, w_ffn_down, l)
        g_next = norm_mix_pre[l + 1] if l + 1 < DEPTH else None
        xr, h = _resid_norm(xr, ff, norm_ffn_post[l], g_next)
    return xr.reshape(b_, t_, d_)
```
